```python
import jax, jax.numpy as jnp
from jax import lax
import numpy as np

D_MODEL = 1024
BATCH = 16
SEQ = 2048
DEPTH = 4
DEC_BATCH = 8
DEC_SEQ = 64
PAST_LEN = 2048

CHUNK = 64
N_EVEN = (DEPTH + 1) // 2
N_ODD = DEPTH // 2
GLA_HEADS = 4
GLA_DV = D_MODEL // 8
GLA_DK = GLA_DV // 2
GLA_RANK = 16
GLA_TAU = 16.0
SB_HEADS = 8
SB_HEAD_DIM = D_MODEL // 16
SB_BLOCK = 128
SGU_WIDTH = D_MODEL
SGU_GROUPS = 4
SGU_CHUNK = 128
MEM_TOKENS = 256
X_HEADS = 4
X_HEAD_DIM = D_MODEL // X_HEADS
EPS = 1e-6

GLA_KW = GLA_HEADS * GLA_DK
GLA_VW = GLA_HEADS * GLA_DV
SB_W = SB_HEADS * SB_HEAD_DIM
MIX_W = GLA_VW + SB_W
EVEN_SPLITS = (GLA_KW, GLA_KW, GLA_VW, GLA_VW, GLA_RANK, SB_W, SB_W, SB_W, SB_W)
EVEN_IN = 2 * GLA_KW + 2 * GLA_VW + GLA_RANK + 4 * SB_W
X_W = X_HEADS * X_HEAD_DIM

kernel_name = 'hybrid_gla_stickbreak_sgu_stream_step'


def rmsnorm(x, g):
    xf = x.astype(jnp.float32)
    y = xf * lax.rsqrt(jnp.mean(xf * xf, axis=-1, keepdims=True) + EPS) * g.astype(jnp.float32)
    return y.astype(x.dtype)


def layernorm(x, g, b):
    xf = x.astype(jnp.float32)
    mu = jnp.mean(xf, axis=-1, keepdims=True)
    xc = xf - mu
    y = xc * lax.rsqrt(jnp.mean(xc * xc, axis=-1, keepdims=True) + EPS) * g.astype(jnp.float32) + b.astype(jnp.float32)
    return y.astype(x.dtype)


def _split(z, sizes):
    idx = np.cumsum(np.array(sizes))[:-1].tolist()
    return jnp.split(z, idx, axis=-1)


def _to_chunks(a, n):
    t = a.shape[1]
    a = jnp.pad(a, ((0, 0), (0, n * CHUNK - t), (0, 0), (0, 0)))
    return a.reshape(a.shape[0], n, CHUNK, a.shape[2], a.shape[3]).transpose(1, 0, 3, 2, 4)


def gla_chunked(q, k, v, log_a, s0):
    b_, t, h, dk = q.shape
    n = -(-t // CHUNK)
    f32 = jnp.float32
    qs = _to_chunks(q.astype(f32) * (dk ** -0.5), n)
    ks = _to_chunks(k.astype(f32), n)
    vs = _to_chunks(v.astype(f32), n)
    gs = _to_chunks(log_a.astype(f32), n)
    causal = jnp.tril(jnp.ones((CHUNK, CHUNK), dtype=bool))

    def step(s, inp):
        qc, kc, vc, gc = inp
        b = jnp.cumsum(gc, axis=2)
        qe = qc * jnp.exp(b)
        ke = kc * jnp.exp(-b)
        att = jnp.where(causal, jnp.einsum('bhtd,bhsd->bhts', qe, ke), 0.0)
        o = jnp.einsum('bhts,bhsv->bhtv', att, vc) + jnp.einsum('bhtd,bhdv->bhtv', qe, s)
        b_last = b[:, :, -1, :]
        s_new = jnp.exp(b_last)[..., None] * s + jnp.einsum('bhsd,bhsv->bhdv', kc * jnp.exp(b_last[:, :, None, :] - b), vc)
        return s_new, o

    s_fin, o = lax.scan(step, s0.astype(f32), (qs, ks, vs, gs))
    o = o.transpose(1, 0, 3, 2, 4).reshape(b_, n * CHUNK, h, v.shape[-1])[:, :t]
    return o.astype(v.dtype), s_fin


def stick_breaking(q, k, v, q_offset):
    tq, d = q.shape[1], q.shape[-1]
    scale = d ** -0.5
    outs = []
    for i0 in range(0, tq, SB_BLOCK):
        i1 = min(i0 + SB_BLOCK, tq)
        kend = q_offset + i1
        kb = k[:, :kend]
        vb = v[:, :kend]
        z = jnp.einsum('bqhd,bkhd->bhqk', q[:, i0:i1], kb).astype(jnp.float32) * scale
        qpos = q_offset + i0 + jnp.arange(i1 - i0)
        kpos = jnp.arange(kend)
        mask = kpos[None, :] < qpos[:, None]
        log_1mb = jnp.where(mask, jax.nn.log_sigmoid(-z), 0.0)
        cum = jnp.cumsum(log_1mb, axis=-1)
        log_w = jax.nn.log_sigmoid(z) + (cum[..., -1:] - cum)
        w = jnp.where(mask, jnp.exp(log_w), 0.0)
        outs.append(jnp.einsum('bhqk,bkhd->bqhd', w, vb.astype(jnp.float32)))
    return jnp.concatenate(outs, axis=1).astype(q.dtype)


def spatial_gating(v_n, w_sp, b_sp):
    bsz, t, _ = v_n.shape
    n = -(-t // SGU_CHUNK)
    vc = jnp.pad(v_n, ((0, 0), (0, n * SGU_CHUNK - t), (0, 0)))
    vc = vc.reshape(bsz, n, SGU_CHUNK, SGU_GROUPS, SGU_WIDTH // SGU_GROUPS)
    w = w_sp * jnp.tril(jnp.ones((SGU_CHUNK, SGU_CHUNK), w_sp.dtype))
    s = jnp.einsum('gts,bnsgc->bntgc', w, vc) + jnp.transpose(b_sp)[None, None, :, :, None]
    return s.reshape(bsz, n * SGU_CHUNK, SGU_WIDTH)[:, :t]


def cross_attend(h, mk, mv, w_cq, w_co):
    bsz, t, _ = h.shape
    q = (h @ w_cq).reshape(bsz, t, X_HEADS, X_HEAD_DIM)
    s = jnp.einsum('bqhd,bmhd->bhqm', q, mk).astype(jnp.float32) * (X_HEAD_DIM ** -0.5)
    p = jax.nn.softmax(s, axis=-1)
    o = jnp.einsum('bhqm,bmhd->bqhd', p, mv.astype(jnp.float32)).astype(h.dtype)
    return o.reshape(bsz, t, X_W) @ w_co


def run_group(x, past_k, past_v, gla_s0, mem_k, mem_v, g_mix, w_in_even, w_alpha, b_alpha, g_gla_out,
              w_out_even, w_in_odd, g_sgu_v, b_sgu_v, w_sp, b_sp, w_out_odd, g_cross, w_cq, w_co, g_final):
    bsz, t, _ = x.shape
    past_len = past_k.shape[2]
    sb_k, sb_v, gla_s, sgu_v = [], [], [], []
    for l in range(DEPTH):
        h = rmsnorm(x, g_mix[l])
        i = l // 2
        if l % 2 == 0:
            qa, ka, va, ga, ra, qb, kb, vb, gb = _split(h @ w_in_even[i], EVEN_SPLITS)
            log_a = jax.nn.log_sigmoid((ra @ w_alpha[i] + b_alpha[i]).astype(jnp.float32)) / GLA_TAU
            oa, s_new = gla_chunked(qa.reshape(bsz, t, GLA_HEADS, GLA_DK), ka.reshape(bsz, t, GLA_HEADS, GLA_DK),
                                    va.reshape(bsz, t, GLA_HEADS, GLA_DV), log_a.reshape(bsz, t, GLA_HEADS, GLA_DK),
                                    gla_s0[i])
            oa = rmsnorm(oa, g_gla_out[i].reshape(GLA_HEADS, GLA_DV)).reshape(bsz, t, GLA_VW) * jax.nn.silu(ga)
            kb = kb.reshape(bsz, t, SB_HEADS, SB_HEAD_DIM)
            vb = vb.reshape(bsz, t, SB_HEADS, SB_HEAD_DIM)
            k_all = jnp.concatenate([past_k[i].astype(kb.dtype), kb], axis=1)
            v_all = jnp.concatenate([past_v[i].astype(vb.dtype), vb], axis=1)
            ob = stick_breaking(qb.reshape(bsz, t, SB_HEADS, SB_HEAD_DIM), k_all, v_all, past_len)
            ob = ob.reshape(bsz, t, SB_W) * jax.nn.silu(gb)
            x = x + jnp.concatenate([oa, ob], axis=-1) @ w_out_even[i]
            sb_k.append(kb)
            sb_v.append(vb)
            gla_s.append(s_new)
        else:
            u, v, g = _split(h @ w_in_odd[i], (SGU_WIDTH, SGU_WIDTH, SGU_WIDTH))
            v_n = layernorm(jax.nn.gelu(v), g_sgu_v[i], b_sgu_v[i])
            s = spatial_gating(v_n, w_sp[i], b_sp[i])
            x = x + (jax.nn.gelu(u) * s * jax.nn.silu(g)) @ w_out_odd[i]
            sgu_v.append(v_n)
        x = x + cross_attend(rmsnorm(x, g_cross[l]), mem_k[l], mem_v[l], w_cq[l], w_co[l])
    return rmsnorm(x, g_final), jnp.stack(sb_k), jnp.stack(sb_v), jnp.stack(gla_s), jnp.stack(sgu_v)


def setup_inputs(seed: int = 0) -> dict:
    key = jax.random.key(seed)
    ks = jax.random.split(key, 27)
    f32 = jnp.float32

    def nrm(k, shape, scale=1.0):
        return jax.random.normal(k, shape, f32) * scale

    def gain(k, shape):
        return 1.0 + 0.01 * jax.random.normal(k, shape, f32)

    return {
        'x_prompt': nrm(ks[0], (BATCH, SEQ, D_MODEL)),
        'x_sample': nrm(ks[1], (DEC_BATCH, DEC_SEQ, D_MODEL)),
        'cache_sb_k': nrm(ks[2], (N_EVEN, DEC_BATCH, PAST_LEN, SB_HEADS, SB_HEAD_DIM)),
        'cache_sb_v': nrm(ks[3], (N_EVEN, DEC_BATCH, PAST_LEN, SB_HEADS, SB_HEAD_DIM)),
        'state_gla': nrm(ks[4], (N_EVEN, DEC_BATCH, GLA_HEADS, GLA_DK, GLA_DV), 0.5),
        'cache_mem_k': nrm(ks[5], (DEPTH, DEC_BATCH, MEM_TOKENS, X_HEADS, X_HEAD_DIM)),
        'cache_mem_v': nrm(ks[6], (DEPTH, DEC_BATCH, MEM_TOKENS, X_HEADS, X_HEAD_DIM)),
        'mem_prompt': nrm(ks[7], (BATCH, MEM_TOKENS, D_MODEL)),
        'g_mix': gain(ks[8], (DEPTH, D_MODEL)),
        'w_in_even': nrm(ks[9], (N_EVEN, D_MODEL, EVEN_IN), D_MODEL ** -0.5),
        'w_alpha': nrm(ks[10], (N_EVEN, GLA_RANK, GLA_KW), GLA_RANK ** -0.5),
        'b_alpha': nrm(ks[11], (N_EVEN, GLA_KW), 0.1),
        'g_gla_out': gain(ks[12], (N_EVEN, GLA_VW)),
        'w_out_even': nrm(ks[13], (N_EVEN, MIX_W, D_MODEL), MIX_W ** -0.5),
        'w_in_odd': nrm(ks[14], (N_ODD, D_MODEL, 3 * SGU_WIDTH), D_MODEL ** -0.5),
        'g_sgu_v': gain(ks[15], (N_ODD, SGU_WIDTH)),
        'b_sgu_v': nrm(ks[16], (N_ODD, SGU_WIDTH), 0.01),
        'w_sp': nrm(ks[17], (N_ODD, SGU_GROUPS, SGU_CHUNK, SGU_CHUNK), SGU_CHUNK ** -0.5),
        'b_sp': gain(ks[18], (N_ODD, SGU_GROUPS, SGU_CHUNK)),
        'w_out_odd': nrm(ks[19], (N_ODD, SGU_WIDTH, D_MODEL), SGU_WIDTH ** -0.5),
        'g_cross': gain(ks[20], (DEPTH, D_MODEL)),
        'g_mem': gain(ks[21], (DEPTH, D_MODEL)),
        'w_cq': nrm(ks[22], (DEPTH, D_MODEL, X_W), D_MODEL ** -0.5),
        'w_ck': nrm(ks[23], (DEPTH, D_MODEL, X_W), D_MODEL ** -0.5),
        'w_cv': nrm(ks[24], (DEPTH, D_MODEL, X_W), D_MODEL ** -0.5),
        'w_co': nrm(ks[25], (DEPTH, X_W, D_MODEL), X_W ** -0.5),
        'g_final': gain(ks[26], (D_MODEL,)),
    }


def reference(x_prompt, x_sample, cache_sb_k, cache_sb_v, state_gla, cache_mem_k, cache_mem_v, mem_prompt,
              g_mix, w_in_even, w_alpha, b_alpha, g_gla_out, w_out_even, w_in_odd, g_sgu_v, b_sgu_v, w_sp, b_sp,
              w_out_odd, g_cross, g_mem, w_cq, w_ck, w_cv, w_co, g_final):
    bp = x_prompt.shape[0]
    n_mem = mem_prompt.shape[1]
    mk_list, mv_list = [], []
    for l in range(DEPTH):
        m = rmsnorm(mem_prompt, g_mem[l])
        mk_list.append((m @ w_ck[l]).reshape(bp, n_mem, X_HEADS, X_HEAD_DIM))
        mv_list.append((m @ w_cv[l]).reshape(bp, n_mem, X_HEADS, X_HEAD_DIM))
    mem_k_prompt = jnp.stack(mk_list)
    mem_v_prompt = jnp.stack(mv_list)

    empty_kv = jnp.zeros((N_EVEN, bp, 0, SB_HEADS, SB_HEAD_DIM), x_prompt.dtype)
    gla_zero = jnp.zeros((N_EVEN, bp, GLA_HEADS, GLA_DK, GLA_DV), jnp.float32)
    y_prompt, sb_k_prompt, sb_v_prompt, gla_prompt, _ = run_group(
        x_prompt, empty_kv, empty_kv, gla_zero, mem_k_prompt, mem_v_prompt, g_mix, w_in_even, w_alpha, b_alpha,
        g_gla_out, w_out_even, w_in_odd, g_sgu_v, b_sgu_v, w_sp, b_sp, w_out_odd, g_cross, w_cq, w_co, g_final)

    y_sample, sb_k_sample, sb_v_sample, gla_sample, sgu_v_sample = run_group(
        x_sample, cache_sb_k, cache_sb_v, state_gla, cache_mem_k, cache_mem_v, g_mix, w_in_even, w_alpha, b_alpha,
        g_gla_out, w_out_even, w_in_odd, g_sgu_v, b_sgu_v, w_sp, b_sp, w_out_odd, g_cross, w_cq, w_co, g_final)

    return (y_prompt, y_sample, sb_k_prompt, sb_v_prompt, gla_prompt, mem_k_prompt, mem_v_prompt,
            sb_k_sample, sb_v_sample, gla_sample, sgu_v_sample)
```

```python
import functools

import jax
import jax.numpy as jnp
from jax import lax
from jax.experimental import pallas as pl
from jax.experimental.pallas import tpu as pltpu

F32 = jnp.float32
BF16 = jnp.bfloat16

EPS = 1e-6
GLA_HEADS = 4
GLA_DK = 64
GLA_DV = 128
GLA_RANK = 16
GLA_TAU = 16.0
GLA_CHUNK = 64
GLA_KW = GLA_HEADS * GLA_DK
GLA_VW = GLA_HEADS * GLA_DV
SB_HEADS = 8
SB_DIM = 64
SB_W = SB_HEADS * SB_DIM
SB_KEYS = 128
SGU_GROUPS = 4
SGU_CHUNK = 128
X_HEADS = 4
LANES = 128
RA_PAD = LANES
VMEM_LIMIT = 56 * 1024 * 1024

C_QA, C_KA, C_VA, C_GA = 0, 256, 512, 1024
C_QB, C_KB, C_VB, C_GB, C_RA, C_END = 1536, 2048, 2560, 3072, 3584, 3712


def _dot(a, b):
    return jnp.dot(a, b, preferred_element_type=F32)


def _dot_nt(a, b):
    return lax.dot_general(a, b, (((1,), (1,)), ((), ())), preferred_element_type=F32)


def _split2(a):
    hi = a.astype(BF16)
    lo = (a - hi.astype(F32)).astype(BF16)
    return hi, lo


def _rms(x, g):
    return x * lax.rsqrt(jnp.mean(x * x, axis=-1, keepdims=True) + EPS) * g


def _sigmoid(x):
    return 1.0 / (1.0 + jnp.exp(-x))


def _silu(x):
    return x * _sigmoid(x)


def _gelu(x):
    return x * (0.5 * (1.0 + jnp.tanh(0.7978845608028654 * (x + 0.044715 * (x * x * x)))))


def _softplus_neg_abs(z):
    return jnp.log1p(jnp.exp(-jnp.abs(z)))


def _iota(shape, dim):
    return lax.broadcasted_iota(jnp.int32, shape, dim)


def _cross_attend(x, g_cross, wq_ref, wo_ref, mkb_ref, mvb_ref):
    d = x.shape[-1]
    hd = d // X_HEADS
    hc = _rms(x, g_cross).astype(BF16)
    q = _dot(hc, wq_ref[...]) * (hd ** -0.5)
    outs = []
    for h in range(X_HEADS):
        sl = slice(h * hd, (h + 1) * hd)
        s = _dot_nt(q[:, sl].astype(BF16), mkb_ref[:, sl])
        e = jnp.exp(s - jnp.max(s, axis=-1, keepdims=True))
        l = jnp.sum(e, axis=-1, keepdims=True)
        outs.append(_dot(e.astype(BF16), mvb_ref[:, sl]) / l)
    o = jnp.concatenate(outs, axis=-1).astype(BF16)
    return x + _dot(o, wo_ref[...])


def _memkv_kernel(m_ref, g_ref, wk_ref, wv_ref, k_ref, v_ref):
    m = _rms(m_ref[0], g_ref[0]).astype(BF16)
    k_ref[0, 0] = _dot(m, wk_ref[0])
    v_ref[0, 0] = _dot(m, wv_ref[0])


def _memkv(mem, g_mem, w_ck, w_cv):
    b, n, d = mem.shape
    depth = g_mem.shape[0]
    out = jax.ShapeDtypeStruct((depth, b, n, d), F32)
    wspec = pl.BlockSpec((1, d, d), lambda l, i: (l, 0, 0))
    ospec = pl.BlockSpec((1, 1, n, d), lambda l, i: (l, i, 0, 0))
    return pl.pallas_call(
        _memkv_kernel,
        grid=(depth, b),
        in_specs=[pl.BlockSpec((1, n, d), lambda l, i: (i, 0, 0)),
                  pl.BlockSpec((1, 1, d), lambda l, i: (l, 0, 0)),
                  wspec, wspec],
        out_specs=[ospec, ospec],
        out_shape=[out, out],
        compiler_params=pltpu.CompilerParams(
            dimension_semantics=("arbitrary", "arbitrary"), vmem_limit_bytes=VMEM_LIMIT),
        name="mem_kv",
    )(mem, g_mem.reshape(depth, 1, d), w_ck, w_cv)


def _even_kernel(*refs, tq, past, has_state):
    it = iter(refs)
    x_ref = next(it)
    pk_ref = next(it) if past else None
    pv_ref = next(it) if past else None
    s0_ref = next(it) if has_state else None
    mk_ref, mv_ref = next(it), next(it)
    gmix_ref, w_ref, wal_ref, bal_ref, ggla_ref, wout_ref = (next(it) for _ in range(6))
    gcr_ref, wq_ref, wo_ref = next(it), next(it), next(it)
    xo_ref, ko_ref, vo_ref, so_ref = next(it), next(it), next(it), next(it)
    kscr, vscr, mkb, mvb, sst, qm, acc, car = (next(it) for _ in range(8))

    t = pl.program_id(1)
    nt = pl.num_programs(1)

    @pl.when(t == 0)
    def _init():
        mkb[...] = mk_ref[0].astype(BF16)
        mvb[...] = mv_ref[0].astype(BF16)
        if past:
            kscr[0:past, :] = pk_ref[0].astype(BF16)
            vscr[0:past, :] = pv_ref[0].astype(BF16)
        if has_state:
            for h in range(GLA_HEADS):
                sst[h * GLA_DK:(h + 1) * GLA_DK, :] = s0_ref[0, h]
        else:
            sst[...] = jnp.zeros(sst.shape, F32)

    x = x_ref[0]
    h = _rms(x, gmix_ref[...]).astype(BF16)

    def proj(c0, c1):
        return _dot(h, w_ref[:, c0:c1])

    qa = proj(C_QA, C_KA) * (GLA_DK ** -0.5)
    ka = proj(C_KA, C_VA)
    va = proj(C_VA, C_GA)
    ga = proj(C_GA, C_QB)
    ra = proj(C_RA, C_END).astype(BF16)
    za = _dot(ra, wal_ref[...]) + bal_ref[...]
    log_a = (jnp.minimum(za, 0.0) - _softplus_neg_abs(za)) * (1.0 / GLA_TAU)

    cl = GLA_CHUNK
    ltri = jnp.where(_iota((cl, cl), 1) <= _iota((cl, cl), 0), 1.0, 0.0).astype(BF16)
    causal = _iota((cl, cl), 1) <= _iota((cl, cl), 0)
    lane_kw = _iota((1, GLA_KW), 1)
    oa_rows = []
    for c in range(tq // cl):
        rs = slice(c * cl, (c + 1) * cl)
        gc, qc, kc, vc = log_a[rs], qa[rs], ka[rs], va[rs]
        g_hi, g_lo = _split2(gc)
        b = _dot(ltri, g_hi) + _dot(ltri, g_lo)
        bt = b.T
        b_last = bt[:, cl - 1:cl]
        qe = qc * jnp.exp(b)
        ke = (kc * jnp.exp(-b)).astype(BF16)
        kdt = (kc.T * jnp.exp(b_last - bt)).astype(BF16)
        decay = jnp.exp(b_last)
        s_all = sst[...]
        s_bf = s_all.astype(BF16)
        vcb = vc.astype(BF16)
        o_heads = []
        for hh in range(GLA_HEADS):
            ks = slice(hh * GLA_DK, (hh + 1) * GLA_DK)
            vs = slice(hh * GLA_DV, (hh + 1) * GLA_DV)
            qeh = jnp.where((lane_kw >= hh * GLA_DK) & (lane_kw < (hh + 1) * GLA_DK), qe, 0.0).astype(BF16)
            att = jnp.where(causal, _dot_nt(qeh, ke), 0.0).astype(BF16)
            o_heads.append(_dot(att, vcb[:, vs]) + _dot(qeh, s_bf))
            sst[ks, :] = decay[ks] * s_all[ks] + _dot(kdt[ks], vcb[:, vs])
        oa_rows.append(jnp.concatenate(o_heads, axis=-1))
    oa = jnp.concatenate(oa_rows, axis=0) if len(oa_rows) > 1 else oa_rows[0]
    ggla = ggla_ref[...]
    oa_n = []
    for hh in range(GLA_HEADS):
        vs = slice(hh * GLA_DV, (hh + 1) * GLA_DV)
        oa_n.append(_rms(oa[:, vs], ggla[:, vs]))
    oa = jnp.concatenate(oa_n, axis=-1) * _silu(ga)

    @pl.when(t == nt - 1)
    def _emit_state():
        for hh in range(GLA_HEADS):
            so_ref[0, hh] = sst[hh * GLA_DK:(hh + 1) * GLA_DK, :]

    qb = proj(C_QB, C_KB) * (SB_DIM ** -0.5)
    kb = proj(C_KB, C_VB)
    vb = proj(C_VB, C_GB)
    gb = proj(C_GB, C_RA)
    ko_ref[0] = kb
    vo_ref[0] = vb
    row0 = pl.multiple_of(past + t * tq, tq)
    kscr[pl.ds(row0, tq), :] = kb.astype(BF16)
    vscr[pl.ds(row0, tq), :] = vb.astype(BF16)

    lane = _iota((1, LANES), 1)
    for p in range(SB_HEADS // 2):
        qp = qb[:, p * LANES:(p + 1) * LANES]
        qm[2 * p] = jnp.where(lane < SB_DIM, qp, 0.0).astype(BF16)
        qm[2 * p + 1] = jnp.where(lane >= SB_DIM, qp, 0.0).astype(BF16)
    acc[...] = jnp.zeros(acc.shape, F32)
    car[...] = jnp.zeros(car.shape, F32)

    def suffix_matrix(kw):
        r = _iota((kw, LANES + kw), 0)
        c = _iota((kw, LANES + kw), 1)
        return jnp.where((c < LANES) | (r > c - LANES), 1.0, 0.0).astype(BF16)

    def sb_block(r0, kw, ue, mask):
        for p in range(SB_HEADS // 2):
            ls_ = slice(p * LANES, (p + 1) * LANES)
            kp = kscr[pl.ds(r0, kw), ls_]
            vp = vscr[pl.ds(r0, kw), ls_]
            for e in range(2):
                hh = 2 * p + e
                z = _dot_nt(qm[hh], kp)
                sp = _softplus_neg_abs(z)
                log_b = jnp.minimum(z, 0.0) - sp
                log_1mb = -jnp.maximum(z, 0.0) - sp
                if mask is not None:
                    log_1mb = jnp.where(mask, log_1mb, 0.0)
                hi, lo = _split2(log_1mb)
                sc = _dot(hi, ue) + _dot(lo, ue)
                carry = car[hh]
                log_w = log_b + sc[:, LANES:] + carry[:, :kw]
                w = jnp.exp(log_w)
                if mask is not None:
                    w = jnp.where(mask, w, 0.0)
                acc[hh] += _dot(w.astype(BF16), vp)
                car[hh] = carry + sc[:, :LANES]

    kd = min(SB_KEYS, tq)
    ue_d = suffix_matrix(kd)
    for dblk in reversed(range(tq // kd)):
        mask = (dblk * kd + _iota((tq, kd), 1)) < _iota((tq, kd), 0)
        sb_block(pl.multiple_of(row0 + dblk * kd, kd), kd, ue_d, mask)

    ue_f = ue_d if kd == SB_KEYS else suffix_matrix(SB_KEYS)
    nblk = (past + t * tq) // SB_KEYS

    def below(i, carry_):
        sb_block(pl.multiple_of((nblk - 1 - i) * SB_KEYS, SB_KEYS), SB_KEYS, ue_f, None)
        return carry_

    lax.fori_loop(0, nblk, below, 0)

    ob = jnp.concatenate(
        [jnp.where(lane < SB_DIM, acc[2 * p], acc[2 * p + 1]) for p in range(SB_HEADS // 2)], axis=-1)
    ob = ob * _silu(gb)

    mix = jnp.concatenate([oa, ob], axis=-1).astype(BF16)
    x = x + _dot(mix, wout_ref[...])
    xo_ref[0] = _cross_attend(x, gcr_ref[...], wq_ref, wo_ref, mkb, mvb)


def _const_spec(shape):
    nd = len(shape)
    return pl.BlockSpec(shape, lambda b, t: (0,) * nd, pipeline_mode=pl.Buffered(1))


def _even_layer(x, past_k, past_v, s0, mem_k, mem_v, wts, tq):
    bsz, t, d = x.shape
    past = 0 if past_k is None else past_k.shape[1]
    has_state = s0 is not None
    nmem = mem_k.shape[1]
    assert t % tq == 0 and tq % GLA_CHUNK == 0 and past % SB_KEYS == 0
    assert tq % min(SB_KEYS, tq) == 0 and (tq % SB_KEYS == 0 or t == tq)

    row = lambda b, i: (b, 0, 0)
    tile = lambda b, i: (b, i, 0)
    args, specs = [x], [pl.BlockSpec((1, tq, d), tile)]
    if past:
        args += [past_k, past_v]
        specs += [pl.BlockSpec((1, past, SB_W), row)] * 2
    if has_state:
        args.append(s0)
        specs.append(pl.BlockSpec((1, GLA_HEADS, GLA_DK, GLA_DV), lambda b, i: (b, 0, 0, 0)))
    args += [mem_k, mem_v]
    specs += [pl.BlockSpec((1, nmem, d), row)] * 2
    for name in ("g_mix", "w_in", "w_alpha", "b_alpha", "g_gla", "w_out", "g_cross", "w_cq", "w_co"):
        args.append(wts[name])
        specs.append(_const_spec(wts[name].shape))

    out_shape = [jax.ShapeDtypeStruct((bsz, t, d), F32),
                 jax.ShapeDtypeStruct((bsz, t, SB_W), F32),
                 jax.ShapeDtypeStruct((bsz, t, SB_W), F32),
                 jax.ShapeDtypeStruct((bsz, GLA_HEADS, GLA_DK, GLA_DV), F32)]
    out_specs = [pl.BlockSpec((1, tq, d), tile),
                 pl.BlockSpec((1, tq, SB_W), tile),
                 pl.BlockSpec((1, tq, SB_W), tile),
                 pl.BlockSpec((1, GLA_HEADS, GLA_DK, GLA_DV), lambda b, i: (b, 0, 0, 0))]
    scratch = [pltpu.VMEM((past + t, SB_W), BF16), pltpu.VMEM((past + t, SB_W), BF16),
               pltpu.VMEM((nmem, d), BF16), pltpu.VMEM((nmem, d), BF16),
               pltpu.VMEM((GLA_KW, GLA_DV), F32),
               pltpu.VMEM((SB_HEADS, tq, LANES), BF16),
               pltpu.VMEM((SB_HEADS, tq, LANES), F32),
               pltpu.VMEM((SB_HEADS, tq, LANES), F32)]
    return pl.pallas_call(
        functools.partial(_even_kernel, tq=tq, past=past, has_state=has_state),
        grid=(bsz, t // tq),
        in_specs=specs, out_specs=out_specs, out_shape=out_shape, scratch_shapes=scratch,
        compiler_params=pltpu.CompilerParams(
            dimension_semantics=("arbitrary", "arbitrary"), vmem_limit_bytes=VMEM_LIMIT),
        name="even_layer",
    )(*args)


def _odd_kernel(*refs, tq, cl, emit_vn, final):
    it = iter(refs)
    x_ref, mk_ref, mv_ref = next(it), next(it), next(it)
    gmix_ref, w_ref, gv_ref, bv_ref, wsp_ref, bsp_ref, wout_ref = (next(it) for _ in range(7))
    gcr_ref, wq_ref, wo_ref = next(it), next(it), next(it)
    gfin_ref = next(it) if final else None
    xo_ref = next(it)
    vn_ref = next(it) if emit_vn else None
    mkb, mvb = next(it), next(it)

    @pl.when(pl.program_id(1) == 0)
    def _init():
        mkb[...] = mk_ref[0].astype(BF16)
        mvb[...] = mv_ref[0].astype(BF16)

    x = x_ref[0]
    d = x.shape[-1]
    h = _rms(x, gmix_ref[...]).astype(BF16)
    u = _dot(h, w_ref[:, 0:d])
    v = _dot(h, w_ref[:, d:2 * d])
    g = _dot(h, w_ref[:, 2 * d:3 * d])

    gv = _gelu(v)
    xc = gv - jnp.mean(gv, axis=-1, keepdims=True)
    v_n = xc * lax.rsqrt(jnp.mean(xc * xc, axis=-1, keepdims=True) + EPS) * gv_ref[...] + bv_ref[...]
    if emit_vn:
        vn_ref[0] = v_n

    gw = d // SGU_GROUPS
    keep = _iota((cl, cl), 1) <= _iota((cl, cl), 0)
    vnb = v_n.astype(BF16)
    bsp = bsp_ref[...]
    wgs = [jnp.where(keep, wsp_ref[gi], 0.0).astype(BF16) for gi in range(SGU_GROUPS)]
    rows = []
    for c in range(tq // cl):
        cols = []
        for gi in range(SGU_GROUPS):
            cols.append(_dot(wgs[gi], vnb[c * cl:(c + 1) * cl, gi * gw:(gi + 1) * gw]) + bsp[:, gi:gi + 1])
        rows.append(jnp.concatenate(cols, axis=-1))
    s = jnp.concatenate(rows, axis=0) if len(rows) > 1 else rows[0]

    y = (_gelu(u) * s * _silu(g)).astype(BF16)
    x = x + _dot(y, wout_ref[...])
    x = _cross_attend(x, gcr_ref[...], wq_ref, wo_ref, mkb, mvb)
    if final:
        x = _rms(x, gfin_ref[...])
    xo_ref[0] = x


def _odd_layer(x, mem_k, mem_v, wts, tq, emit_vn, g_final):
    bsz, t, d = x.shape
    nmem = mem_k.shape[1]
    cl = min(SGU_CHUNK, t)
    final = g_final is not None
    assert t % tq == 0 and tq % cl == 0

    row = lambda b, i: (b, 0, 0)
    tile = lambda b, i: (b, i, 0)
    args = [x, mem_k, mem_v]
    specs = [pl.BlockSpec((1, tq, d), tile)] + [pl.BlockSpec((1, nmem, d), row)] * 2
    consts = [wts["g_mix"], wts["w_in"], wts["g_sgu"], wts["b_sgu"], wts["w_sp"][:, :cl, :cl],
              wts["b_sp_t"][:cl], wts["w_out"], wts["g_cross"], wts["w_cq"], wts["w_co"]]
    if final:
        consts.append(g_final)
    for a in consts:
        args.append(a)
        specs.append(_const_spec(a.shape))

    out_shape = [jax.ShapeDtypeStruct((bsz, t, d), F32)]
    out_specs = [pl.BlockSpec((1, tq, d), tile)]
    if emit_vn:
        out_shape.append(jax.ShapeDtypeStruct((bsz, t, d), F32))
        out_specs.append(pl.BlockSpec((1, tq, d), tile))
    return pl.pallas_call(
        functools.partial(_odd_kernel, tq=tq, cl=cl, emit_vn=emit_vn, final=final),
        grid=(bsz, t // tq),
        in_specs=specs, out_specs=out_specs, out_shape=out_shape,
        scratch_shapes=[pltpu.VMEM((nmem, d), BF16), pltpu.VMEM((nmem, d), BF16)],
        compiler_params=pltpu.CompilerParams(
            dimension_semantics=("arbitrary", "arbitrary"), vmem_limit_bytes=VMEM_LIMIT),
        name="odd_layer",
    )(*args)


def _tile_rows(t):
    return min(256, t)


def _run_group(x, past_k, past_v, s0, mem_k, mem_v, even_w, odd_w, g_final, emit_vn):
    depth = mem_k.shape[0]
    bsz, t, d = x.shape
    tq = _tile_rows(t)
    mem_k = mem_k.reshape(depth, bsz, -1, d)
    mem_v = mem_v.reshape(depth, bsz, -1, d)
    sb_k, sb_v, gla_s, sgu_v = [], [], [], []
    for l in range(depth):
        i = l // 2
        if l % 2 == 0:
            pk = None if past_k is None else past_k[i].reshape(bsz, -1, SB_W)
            pv = None if past_v is None else past_v[i].reshape(bsz, -1, SB_W)
            st = None if s0 is None else s0[i]
            x, kb, vb, s_new = _even_layer(x, pk, pv, st, mem_k[l], mem_v[l], even_w[l], tq)
            sb_k.append(kb.reshape(bsz, t, SB_HEADS, SB_DIM))
            sb_v.append(vb.reshape(bsz, t, SB_HEADS, SB_DIM))
            gla_s.append(s_new)
        else:
            outs = _odd_layer(x, mem_k[l], mem_v[l], odd_w[l], tq, emit_vn,
                              g_final if l == depth - 1 else None)
            x = outs[0]
            if emit_vn:
                sgu_v.append(outs[1])
    return x, jnp.stack(sb_k), jnp.stack(sb_v), jnp.stack(gla_s), (jnp.stack(sgu_v) if emit_vn else None)


def kernel(x_prompt, x_sample, cache_sb_k, cache_sb_v, state_gla, cache_mem_k, cache_mem_v, mem_prompt, g_mix, w_in_even, w_alpha, b_alpha, g_gla_out, w_out_even, w_in_odd, g_sgu_v, b_sgu_v, w_sp, b_sp, w_out_odd, g_cross, g_mem, w_cq, w_ck, w_cv, w_co, g_final):
    depth, d = g_mix.shape
    bp, n_mem, _ = mem_prompt.shape
    row = lambda a: a.reshape(1, -1)

    even_w, odd_w = {}, {}
    for l in range(depth):
        i = l // 2
        shared = dict(g_mix=row(g_mix[l]), g_cross=row(g_cross[l]),
                      w_cq=w_cq[l].astype(BF16), w_co=w_co[l].astype(BF16))
        if l % 2 == 0:
            w = w_in_even[i]
            o_ra = C_QB
            w_main = jnp.concatenate(
                [w[:, :o_ra], w[:, o_ra + GLA_RANK:], w[:, o_ra:o_ra + GLA_RANK],
                 jnp.zeros((d, RA_PAD - GLA_RANK), w.dtype)], axis=1).astype(BF16)
            w_al = jnp.concatenate(
                [w_alpha[i], jnp.zeros((RA_PAD - GLA_RANK, GLA_KW), w_alpha.dtype)], axis=0).astype(BF16)
            even_w[l] = dict(shared, w_in=w_main, w_alpha=w_al, b_alpha=row(b_alpha[i]),
                             g_gla=row(g_gla_out[i]), w_out=w_out_even[i].astype(BF16))
        else:
            odd_w[l] = dict(shared, w_in=w_in_odd[i].astype(BF16), g_sgu=row(g_sgu_v[i]),
                            b_sgu=row(b_sgu_v[i]), w_sp=w_sp[i], b_sp_t=b_sp[i].T,
                            w_out=w_out_odd[i].astype(BF16))

    mem_k_p, mem_v_p = _memkv(mem_prompt, g_mem, w_ck.astype(BF16), w_cv.astype(BF16))
    gfin = row(g_final)

    y_p, sbk_p, sbv_p, gla_p, _ = _run_group(
        x_prompt, None, None, None, mem_k_p, mem_v_p, even_w, odd_w, gfin, False)
    y_s, sbk_s, sbv_s, gla_s, sgu_s = _run_group(
        x_sample, cache_sb_k, cache_sb_v, state_gla, cache_mem_k, cache_mem_v, even_w, odd_w, gfin, True)

    mshape = (depth, bp, n_mem, X_HEADS, d // X_HEADS)
    return (y_p, y_s, sbk_p, sbv_p, gla_p, mem_k_p.reshape(mshape), mem_v_p.reshape(mshape),
            sbk_s, sbv_s, gla_s, sgu_s)
```

```python
import functools

import jax
import jax.numpy as jnp
from jax import lax
from jax.experimental import pallas as pl
from jax.experimental.pallas import tpu as pltpu

F32 = jnp.float32
BF16 = jnp.bfloat16

EPS = 1e-6
GLA_HEADS = 4
GLA_DK = 64
GLA_DV = 128
GLA_RANK = 16
GLA_TAU = 16.0
GLA_CHUNK = 64
GLA_KW = GLA_HEADS * GLA_DK
GLA_VW = GLA_HEADS * GLA_DV
SB_HEADS = 8
SB_DIM = 64
SB_W = SB_HEADS * SB_DIM
SB_KEYS = 128
SGU_GROUPS = 4
SGU_CHUNK = 128
X_HEADS = 4
LANES = 128
RA_PAD = LANES
VMEM_LIMIT = 56 * 1024 * 1024

C_QA, C_KA, C_VA, C_GA = 0, 256, 512, 1024
C_QB, C_KB, C_VB, C_GB, C_RA, C_END = 1536, 2048, 2560, 3072, 3584, 3712


def _dot(a, b):
    return jnp.dot(a, b, preferred_element_type=F32)


def _dot_nt(a, b):
    return lax.dot_general(a, b, (((1,), (1,)), ((), ())), preferred_element_type=F32)


def _split2(a):
    hi = a.astype(BF16)
    lo = (a - hi.astype(F32)).astype(BF16)
    return hi, lo


def _rms(x, g):
    return x * lax.rsqrt(jnp.mean(x * x, axis=-1, keepdims=True) + EPS) * g


def _sigmoid(x):
    return 1.0 / (1.0 + jnp.exp(-x))


def _silu(x):
    return x * _sigmoid(x)


def _gelu(x):
    return x * (0.5 * (1.0 + jnp.tanh(0.7978845608028654 * (x + 0.044715 * (x * x * x)))))


def _softplus_neg_abs(z):
    return jnp.log1p(jnp.exp(-jnp.abs(z)))


def _iota(shape, dim):
    return lax.broadcasted_iota(jnp.int32, shape, dim)


def _cross_attend(x, g_cross, wq_ref, wo_ref, mkb_ref, mvb_ref):
    d = x.shape[-1]
    hd = d // X_HEADS
    hc = _rms(x, g_cross).astype(BF16)
    q = _dot(hc, wq_ref[...]) * (hd ** -0.5)
    outs = []
    for h in range(X_HEADS):
        sl = slice(h * hd, (h + 1) * hd)
        s = _dot_nt(q[:, sl].astype(BF16), mkb_ref[:, sl])
        e = jnp.exp(s - jnp.max(s, axis=-1, keepdims=True))
        l = jnp.sum(e, axis=-1, keepdims=True)
        outs.append(_dot(e.astype(BF16), mvb_ref[:, sl]) / l)
    o = jnp.concatenate(outs, axis=-1).astype(BF16)
    return x + _dot(o, wo_ref[...])


def _memkv_kernel(m_ref, g_ref, wk_ref, wv_ref, k_ref, v_ref):
    m = _rms(m_ref[0], g_ref[0]).astype(BF16)
    k_ref[0, 0] = _dot(m, wk_ref[0])
    v_ref[0, 0] = _dot(m, wv_ref[0])


def _memkv(mem, g_mem, w_ck, w_cv):
    b, n, d = mem.shape
    depth = g_mem.shape[0]
    out = jax.ShapeDtypeStruct((depth, b, n, d), F32)
    wspec = pl.BlockSpec((1, d, d), lambda l, i: (l, 0, 0))
    ospec = pl.BlockSpec((1, 1, n, d), lambda l, i: (l, i, 0, 0))
    return pl.pallas_call(
        _memkv_kernel,
        grid=(depth, b),
        in_specs=[pl.BlockSpec((1, n, d), lambda l, i: (i, 0, 0)),
                  pl.BlockSpec((1, 1, d), lambda l, i: (l, 0, 0)),
                  wspec, wspec],
        out_specs=[ospec, ospec],
        out_shape=[out, out],
        compiler_params=pltpu.CompilerParams(
            dimension_semantics=("arbitrary", "arbitrary"), vmem_limit_bytes=VMEM_LIMIT),
        name="mem_kv",
    )(mem, g_mem.reshape(depth, 1, d), w_ck, w_cv)


def _even_kernel(*refs, tq, nt_static, past, has_state):
    it = iter(refs)
    x_ref = next(it)
    pk_ref = next(it) if past else None
    pv_ref = next(it) if past else None
    s0_ref = next(it) if has_state else None
    mk_ref, mv_ref = next(it), next(it)
    gmix_ref, w_ref, wal_ref, bal_ref, ggla_ref, wout_ref = (next(it) for _ in range(6))
    gcr_ref, wq_ref, wo_ref = next(it), next(it), next(it)
    xo_ref, ko_ref, vo_ref, so_ref = next(it), next(it), next(it), next(it)
    kscr, vscr, mkb, mvb, sst, qs, acc, car = (next(it) for _ in range(8))

    t = pl.program_id(1)
    nt = pl.num_programs(1)
    hist = kscr.shape[1]
    seq = tq * nt_static

    @pl.when(t == 0)
    def _init():
        mkb[...] = mk_ref[0].astype(BF16)
        mvb[...] = mv_ref[0].astype(BF16)
        half0 = (_iota((1, SB_W), 1) % LANES) < SB_DIM
        for e in range(2):
            keep_e = half0 if e == 0 else jnp.logical_not(half0)
            if past:
                kscr[e, 0:past, :] = jnp.where(keep_e, pk_ref[0], 0.0).astype(BF16)
                vscr[e, 0:past, :] = jnp.where(keep_e, pv_ref[0], 0.0).astype(BF16)
            if hist > past + seq:
                kscr[e, past + seq:hist, :] = jnp.zeros((hist - past - seq, SB_W), BF16)
                vscr[e, past + seq:hist, :] = jnp.zeros((hist - past - seq, SB_W), BF16)
        if has_state:
            for h in range(GLA_HEADS):
                sst[h * GLA_DK:(h + 1) * GLA_DK, :] = s0_ref[0, h]
        else:
            sst[...] = jnp.zeros(sst.shape, F32)

    x = x_ref[0]
    h = _rms(x, gmix_ref[...]).astype(BF16)

    def proj(c0, c1):
        return _dot(h, w_ref[:, c0:c1])

    qa = proj(C_QA, C_KA) * (GLA_DK ** -0.5)
    ka = proj(C_KA, C_VA)
    va = proj(C_VA, C_GA)
    ga = proj(C_GA, C_QB)
    ra = proj(C_RA, C_END).astype(BF16)
    za = _dot(ra, wal_ref[...]) + bal_ref[...]
    log_a = (jnp.minimum(za, 0.0) - _softplus_neg_abs(za)) * (1.0 / GLA_TAU)

    cl = GLA_CHUNK
    nc = tq // cl
    row_t = _iota((tq, tq), 0)
    col_t = _iota((tq, tq), 1)
    causal = ((row_t // cl) == (col_t // cl)) & (col_t <= row_t)
    ltri = jnp.where(causal, 1.0, 0.0).astype(BF16)
    g_hi, g_lo = _split2(log_a)
    b = _dot(jnp.concatenate([ltri, ltri], axis=1),
             jnp.concatenate([g_hi, g_lo], axis=0))
    bt = b.T
    lane_t = _iota((1, tq), 1)
    b_last = [bt[:, (ci + 1) * cl - 1:(ci + 1) * cl] for ci in range(nc)]
    bl = jnp.broadcast_to(b_last[0], bt.shape)
    for ci in range(1, nc):
        bl = jnp.where(lane_t >= ci * cl, b_last[ci], bl)
    qe = qa * jnp.exp(b)
    ke = (ka * jnp.exp(-b)).astype(BF16)
    kdt = (ka.T * jnp.exp(bl - bt)).astype(BF16)
    vab = va.astype(BF16)
    lane_kw = _iota((1, GLA_KW), 1)
    zero_b = jnp.zeros((), BF16)
    qeh, o_intra, upd = [], [], []
    for hh in range(GLA_HEADS):
        ks = slice(hh * GLA_DK, (hh + 1) * GLA_DK)
        vs = slice(hh * GLA_DV, (hh + 1) * GLA_DV)
        q_h = jnp.where((lane_kw >= hh * GLA_DK) & (lane_kw < (hh + 1) * GLA_DK), qe, 0.0).astype(BF16)
        qeh.append(q_h)
        att = jnp.where(causal, _dot_nt(q_h, ke), 0.0).astype(BF16)
        o_intra.append(_dot(att, vab[:, vs]))
        kd_h = kdt[ks]
        kd_c = [jnp.where((lane_t >= ci * cl) & (lane_t < (ci + 1) * cl), kd_h, zero_b) for ci in range(nc)]
        upd.append(_dot(jnp.concatenate(kd_c, axis=0) if nc > 1 else kd_c[0], vab[:, vs]))
    s_all = sst[...]
    o_inter = []
    for ci in range(nc):
        rs = slice(ci * cl, (ci + 1) * cl)
        q_c = jnp.concatenate([qeh[hh][rs] for hh in range(GLA_HEADS)], axis=0)
        o_inter.append(_dot(q_c, s_all.astype(BF16)))
        u_c = jnp.concatenate([upd[hh][ci * GLA_DK:(ci + 1) * GLA_DK] for hh in range(GLA_HEADS)], axis=0)
        s_all = jnp.exp(b_last[ci]) * s_all + u_c
    sst[...] = s_all
    o_heads = []
    for hh in range(GLA_HEADS):
        inter = [o_inter[ci][hh * cl:(hh + 1) * cl] for ci in range(nc)]
        o_heads.append(o_intra[hh] + (jnp.concatenate(inter, axis=0) if nc > 1 else inter[0]))
    oa = jnp.concatenate(o_heads, axis=-1)
    ggla = ggla_ref[...]
    oa_n = []
    for hh in range(GLA_HEADS):
        vs = slice(hh * GLA_DV, (hh + 1) * GLA_DV)
        oa_n.append(_rms(oa[:, vs], ggla[:, vs]))
    oa = jnp.concatenate(oa_n, axis=-1) * _silu(ga)

    @pl.when(t == nt - 1)
    def _emit_state():
        for hh in range(GLA_HEADS):
            so_ref[0, hh] = sst[hh * GLA_DK:(hh + 1) * GLA_DK, :]

    qb = proj(C_QB, C_KB) * (SB_DIM ** -0.5)
    kb = proj(C_KB, C_VB)
    vb = proj(C_VB, C_GB)
    gb = proj(C_GB, C_RA)
    ko_ref[0] = kb
    vo_ref[0] = vb
    half = (_iota((1, SB_W), 1) % LANES) < SB_DIM
    row0 = pl.multiple_of(past + t * tq, min(tq, SB_KEYS))
    for e in range(2):
        keep_e = half if e == 0 else jnp.logical_not(half)
        kscr[e, pl.ds(row0, tq), :] = jnp.where(keep_e, kb, 0.0).astype(BF16)
        vscr[e, pl.ds(row0, tq), :] = jnp.where(keep_e, vb, 0.0).astype(BF16)
    qs[...] = qb.astype(BF16)
    acc[...] = jnp.zeros(acc.shape, F32)
    car[...] = jnp.zeros(car.shape, F32)

    ue_r = _iota((2 * SB_KEYS, 2 * LANES), 0) % SB_KEYS
    ue_c = _iota((2 * SB_KEYS, 2 * LANES), 1)
    ue2 = jnp.where((ue_c < LANES) | (ue_r > ue_c - LANES), 1.0, 0.0).astype(BF16)
    n_pairs = SB_HEADS // 2

    def sb_block(r0, rs, masked):
        nr = tq - rs
        keys = pl.ds(r0, SB_KEYS)
        if masked:
            vis1 = _iota((nr, SB_KEYS), 1) < _iota((nr, SB_KEYS), 0)
            vis2 = (_iota((nr, 2 * SB_KEYS), 1) % SB_KEYS) < _iota((nr, 2 * SB_KEYS), 0)
        log_b, lhs = [], []
        for p in range(n_pairs):
            ls_ = slice(p * LANES, (p + 1) * LANES)
            k2 = jnp.concatenate([kscr[0, keys, ls_], kscr[1, keys, ls_]], axis=0)
            z = _dot_nt(qs[rs:tq, ls_], k2)
            sp = jnp.maximum(z, 0.0) + jnp.log(1.0 + jnp.exp(-jnp.abs(z)))
            log_b.append(z - sp)
            if masked:
                sp = jnp.where(vis2, sp, 0.0)
            hi, lo = _split2(sp)
            for e in range(2):
                es = slice(e * SB_KEYS, (e + 1) * SB_KEYS)
                lhs.append(jnp.concatenate([hi[:, es], lo[:, es]], axis=1))
        sc = _dot(jnp.concatenate(lhs, axis=0), ue2)
        for p in range(n_pairs):
            ls_ = slice(p * LANES, (p + 1) * LANES)
            ws = []
            for e in range(2):
                hh = 2 * p + e
                sc_h = sc[hh * nr:(hh + 1) * nr]
                carry = car[hh, rs:tq, :]
                w = jnp.exp(log_b[p][:, e * SB_KEYS:(e + 1) * SB_KEYS] - sc_h[:, LANES:] - carry)
                if masked:
                    w = jnp.where(vis1, w, 0.0)
                ws.append(w.astype(BF16))
                car[hh, rs:tq, :] = carry + sc_h[:, :LANES]
            v2 = jnp.concatenate([vscr[0, keys, ls_], vscr[1, keys, ls_]], axis=0)
            acc[p, rs:tq, :] += _dot(jnp.concatenate(ws, axis=1), v2)

    for dblk in reversed(range(pl.cdiv(tq, SB_KEYS))):
        sb_block(pl.multiple_of(row0 + dblk * SB_KEYS, SB_KEYS), dblk * SB_KEYS, True)

    nblk = (past + t * tq) // SB_KEYS

    def below(i, carry_):
        sb_block(pl.multiple_of((nblk - 1 - i) * SB_KEYS, SB_KEYS), 0, False)
        return carry_

    lax.fori_loop(0, nblk, below, 0)

    ob = jnp.concatenate([acc[p] for p in range(n_pairs)], axis=-1)
    ob = ob * _silu(gb)

    mix = jnp.concatenate([oa, ob], axis=-1).astype(BF16)
    x = x + _dot(mix, wout_ref[...])
    xo_ref[0] = _cross_attend(x, gcr_ref[...], wq_ref, wo_ref, mkb, mvb)


def _const_spec(shape):
    nd = len(shape)
    return pl.BlockSpec(shape, lambda b, t: (0,) * nd, pipeline_mode=pl.Buffered(1))


def _even_layer(x, past_k, past_v, s0, mem_k, mem_v, wts, tq):
    bsz, t, d = x.shape
    past = 0 if past_k is None else past_k.shape[1]
    has_state = s0 is not None
    nmem = mem_k.shape[1]
    assert t % tq == 0 and tq % GLA_CHUNK == 0 and past % SB_KEYS == 0
    assert (tq % SB_KEYS == 0 or t == tq) and SB_KEYS == LANES

    row = lambda b, i: (b, 0, 0)
    tile = lambda b, i: (b, i, 0)
    args, specs = [x], [pl.BlockSpec((1, tq, d), tile)]
    if past:
        args += [past_k, past_v]
        specs += [pl.BlockSpec((1, past, SB_W), row)] * 2
    if has_state:
        args.append(s0)
        specs.append(pl.BlockSpec((1, GLA_HEADS, GLA_DK, GLA_DV), lambda b, i: (b, 0, 0, 0)))
    args += [mem_k, mem_v]
    specs += [pl.BlockSpec((1, nmem, d), row)] * 2
    for name in ("g_mix", "w_in", "w_alpha", "b_alpha", "g_gla", "w_out", "g_cross", "w_cq", "w_co"):
        args.append(wts[name])
        specs.append(_const_spec(wts[name].shape))

    out_shape = [jax.ShapeDtypeStruct((bsz, t, d), F32),
                 jax.ShapeDtypeStruct((bsz, t, SB_W), F32),
                 jax.ShapeDtypeStruct((bsz, t, SB_W), F32),
                 jax.ShapeDtypeStruct((bsz, GLA_HEADS, GLA_DK, GLA_DV), F32)]
    out_specs = [pl.BlockSpec((1, tq, d), tile),
                 pl.BlockSpec((1, tq, SB_W), tile),
                 pl.BlockSpec((1, tq, SB_W), tile),
                 pl.BlockSpec((1, GLA_HEADS, GLA_DK, GLA_DV), lambda b, i: (b, 0, 0, 0))]
    hist = past + pl.cdiv(t, SB_KEYS) * SB_KEYS
    scratch = [pltpu.VMEM((2, hist, SB_W), BF16), pltpu.VMEM((2, hist, SB_W), BF16),
               pltpu.VMEM((nmem, d), BF16), pltpu.VMEM((nmem, d), BF16),
               pltpu.VMEM((GLA_KW, GLA_DV), F32),
               pltpu.VMEM((tq, SB_W), BF16),
               pltpu.VMEM((SB_HEADS // 2, tq, LANES), F32),
               pltpu.VMEM((SB_HEADS, tq, LANES), F32)]
    return pl.pallas_call(
        functools.partial(_even_kernel, tq=tq, nt_static=t // tq, past=past, has_state=has_state),
        grid=(bsz, t // tq),
        in_specs=specs, out_specs=out_specs, out_shape=out_shape, scratch_shapes=scratch,
        compiler_params=pltpu.CompilerParams(
            dimension_semantics=("arbitrary", "arbitrary"), vmem_limit_bytes=VMEM_LIMIT),
        name="even_layer",
    )(*args)


def _odd_kernel(*refs, tq, cl, emit_vn, final):
    it = iter(refs)
    x_ref, mk_ref, mv_ref = next(it), next(it), next(it)
    gmix_ref, w_ref, gv_ref, bv_ref, wsp_ref, bsp_ref, wout_ref = (next(it) for _ in range(7))
    gcr_ref, wq_ref, wo_ref = next(it), next(it), next(it)
    gfin_ref = next(it) if final else None
    xo_ref = next(it)
    vn_ref = next(it) if emit_vn else None
    mkb, mvb = next(it), next(it)

    @pl.when(pl.program_id(1) == 0)
    def _init():
        mkb[...] = mk_ref[0].astype(BF16)
        mvb[...] = mv_ref[0].astype(BF16)

    x = x_ref[0]
    d = x.shape[-1]
    h = _rms(x, gmix_ref[...]).astype(BF16)
    u = _dot(h, w_ref[:, 0:d])
    v = _dot(h, w_ref[:, d:2 * d])
    g = _dot(h, w_ref[:, 2 * d:3 * d])

    gv = _gelu(v)
    xc = gv - jnp.mean(gv, axis=-1, keepdims=True)
    v_n = xc * lax.rsqrt(jnp.mean(xc * xc, axis=-1, keepdims=True) + EPS) * gv_ref[...] + bv_ref[...]
    if emit_vn:
        vn_ref[0] = v_n

    gw = d // SGU_GROUPS
    keep = _iota((cl, cl), 1) <= _iota((cl, cl), 0)
    vnb = v_n.astype(BF16)
    bsp = bsp_ref[...]
    wgs = [jnp.where(keep, wsp_ref[gi], 0.0).astype(BF16) for gi in range(SGU_GROUPS)]
    rows = []
    for c in range(tq // cl):
        cols = []
        for gi in range(SGU_GROUPS):
            cols.append(_dot(wgs[gi], vnb[c * cl:(c + 1) * cl, gi * gw:(gi + 1) * gw]) + bsp[:, gi:gi + 1])
        rows.append(jnp.concatenate(cols, axis=-1))
    s = jnp.concatenate(rows, axis=0) if len(rows) > 1 else rows[0]

    y = (_gelu(u) * s * _silu(g)).astype(BF16)
    x = x + _dot(y, wout_ref[...])
    x = _cross_attend(x, gcr_ref[...], wq_ref, wo_ref, mkb, mvb)
    if final:
        x = _rms(x, gfin_ref[...])
    xo_ref[0] = x


def _odd_layer(x, mem_k, mem_v, wts, tq, emit_vn, g_final):
    bsz, t, d = x.shape
    nmem = mem_k.shape[1]
    cl = min(SGU_CHUNK, t)
    final = g_final is not None
    assert t % tq == 0 and tq % cl == 0

    row = lambda b, i: (b, 0, 0)
    tile = lambda b, i: (b, i, 0)
    args = [x, mem_k, mem_v]
    specs = [pl.BlockSpec((1, tq, d), tile)] + [pl.BlockSpec((1, nmem, d), row)] * 2
    consts = [wts["g_mix"], wts["w_in"], wts["g_sgu"], wts["b_sgu"], wts["w_sp"][:, :cl, :cl],
              wts["b_sp_t"][:cl], wts["w_out"], wts["g_cross"], wts["w_cq"], wts["w_co"]]
    if final:
        consts.append(g_final)
    for a in consts:
        args.append(a)
        specs.append(_const_spec(a.shape))

    out_shape = [jax.ShapeDtypeStruct((bsz, t, d), F32)]
    out_specs = [pl.BlockSpec((1, tq, d), tile)]
    if emit_vn:
        out_shape.append(jax.ShapeDtypeStruct((bsz, t, d), F32))
        out_specs.append(pl.BlockSpec((1, tq, d), tile))
    return pl.pallas_call(
        functools.partial(_odd_kernel, tq=tq, cl=cl, emit_vn=emit_vn, final=final),
        grid=(bsz, t // tq),
        in_specs=specs, out_specs=out_specs, out_shape=out_shape,
        scratch_shapes=[pltpu.VMEM((nmem, d), BF16), pltpu.VMEM((nmem, d), BF16)],
        compiler_params=pltpu.CompilerParams(
            dimension_semantics=("arbitrary", "arbitrary"), vmem_limit_bytes=VMEM_LIMIT),
        name="odd_layer",
    )(*args)


def _tile_rows(t):
    return min(256, t)


def _run_group(x, past_k, past_v, s0, mem_k, mem_v, even_w, odd_w, g_final, emit_vn):
    depth = mem_k.shape[0]
    bsz, t, d = x.shape
    tq = _tile_rows(t)
    mem_k = mem_k.reshape(depth, bsz, -1, d)
    mem_v = mem_v.reshape(depth, bsz, -1, d)
    sb_k, sb_v, gla_s, sgu_v = [], [], [], []
    for l in range(depth):
        i = l // 2
        if l % 2 == 0:
            pk = None if past_k is None else past_k[i].reshape(bsz, -1, SB_W)
            pv = None if past_v is None else past_v[i].reshape(bsz, -1, SB_W)
            st = None if s0 is None else s0[i]
            x, kb, vb, s_new = _even_layer(x, pk, pv, st, mem_k[l], mem_v[l], even_w[l], tq)
            sb_k.append(kb.reshape(bsz, t, SB_HEADS, SB_DIM))
            sb_v.append(vb.reshape(bsz, t, SB_HEADS, SB_DIM))
            gla_s.append(s_new)
        else:
            outs = _odd_layer(x, mem_k[l], mem_v[l], odd_w[l], tq, emit_vn,
                              g_final if l == depth - 1 else None)
            x = outs[0]
            if emit_vn:
                sgu_v.append(outs[1])
    return x, jnp.stack(sb_k), jnp.stack(sb_v), jnp.stack(gla_s), (jnp.stack(sgu_v) if emit_vn else None)


def kernel(x_prompt, x_sample, cache_sb_k, cache_sb_v, state_gla, cache_mem_k, cache_mem_v, mem_prompt, g_mix, w_in_even, w_alpha, b_alpha, g_gla_out, w_out_even, w_in_odd, g_sgu_v, b_sgu_v, w_sp, b_sp, w_out_odd, g_cross, g_mem, w_cq, w_ck, w_cv, w_co, g_final):
    depth, d = g_mix.shape
    bp, n_mem, _ = mem_prompt.shape
    row = lambda a: a.reshape(1, -1)

    even_w, odd_w = {}, {}
    for l in range(depth):
        i = l // 2
        shared = dict(g_mix=row(g_mix[l]), g_cross=row(g_cross[l]),
                      w_cq=w_cq[l].astype(BF16), w_co=w_co[l].astype(BF16))
        if l % 2 == 0:
            w = w_in_even[i]
            o_ra = C_QB
            w_main = jnp.concatenate(
                [w[:, :o_ra], w[:, o_ra + GLA_RANK:], w[:, o_ra:o_ra + GLA_RANK],
                 jnp.zeros((d, RA_PAD - GLA_RANK), w.dtype)], axis=1).astype(BF16)
            w_al = jnp.concatenate(
                [w_alpha[i], jnp.zeros((RA_PAD - GLA_RANK, GLA_KW), w_alpha.dtype)], axis=0).astype(BF16)
            even_w[l] = dict(shared, w_in=w_main, w_alpha=w_al, b_alpha=row(b_alpha[i]),
                             g_gla=row(g_gla_out[i]), w_out=w_out_even[i].astype(BF16))
        else:
            odd_w[l] = dict(shared, w_in=w_in_odd[i].astype(BF16), g_sgu=row(g_sgu_v[i]),
                            b_sgu=row(b_sgu_v[i]), w_sp=w_sp[i], b_sp_t=b_sp[i].T,
                            w_out=w_out_odd[i].astype(BF16))

    mem_k_p, mem_v_p = _memkv(mem_prompt, g_mem, w_ck.astype(BF16), w_cv.astype(BF16))
    gfin = row(g_final)

    y_p, sbk_p, sbv_p, gla_p, _ = _run_group(
        x_prompt, None, None, None, mem_k_p, mem_v_p, even_w, odd_w, gfin, False)
    y_s, sbk_s, sbv_s, gla_s, sgu_s = _run_group(
        x_sample, cache_sb_k, cache_sb_v, state_gla, cache_mem_k, cache_mem_v, even_w, odd_w, gfin, True)

    mshape = (depth, bp, n_mem, X_HEADS, d // X_HEADS)
    return (y_p, y_s, sbk_p, sbv_p, gla_p, mem_k_p.reshape(mshape), mem_v_p.reshape(mshape),
            sbk_s, sbv_s, gla_s, sgu_s)
```

```python
import functools

import jax
import jax.numpy as jnp
from jax import lax
from jax.experimental import pallas as pl
from jax.experimental.pallas import tpu as pltpu

F32 = jnp.float32
BF16 = jnp.bfloat16

EPS = 1e-6
GLA_HEADS = 4
GLA_DK = 64
GLA_DV = 128
GLA_RANK = 16
GLA_TAU = 16.0
GLA_CHUNK = 64
GLA_KW = GLA_HEADS * GLA_DK
GLA_VW = GLA_HEADS * GLA_DV
SB_HEADS = 8
SB_DIM = 64
SB_W = SB_HEADS * SB_DIM
SB_KEYS = 128
SGU_GROUPS = 4
SGU_CHUNK = 128
X_HEADS = 4
LANES = 128
RA_PAD = LANES
VMEM_LIMIT = 56 * 1024 * 1024
LOG2_E = 1.4426950408889634
SB_UNDERFLOW_LOG2 = 160.0
PAST_SLAB = 256

C_QA, C_KA, C_VA, C_GA = 0, 256, 512, 1024
C_QB, C_KB, C_VB, C_GB, C_RA, C_END = 1536, 2048, 2560, 3072, 3584, 3712


def _dot(a, b):
    return jnp.dot(a, b, preferred_element_type=F32)


def _dot_nt(a, b):
    return lax.dot_general(a, b, (((1,), (1,)), ((), ())), preferred_element_type=F32)


def _split2(a):
    hi = a.astype(BF16)
    lo = (a - hi.astype(F32)).astype(BF16)
    return hi, lo


def _rms(x, g):
    return x * lax.rsqrt(jnp.mean(x * x, axis=-1, keepdims=True) + EPS) * g


def _sigmoid(x):
    return 1.0 / (1.0 + jnp.exp(-x))


def _silu(x):
    return x * _sigmoid(x)


def _gelu(x):
    return x * (0.5 * (1.0 + jnp.tanh(0.7978845608028654 * (x + 0.044715 * (x * x * x)))))


def _softplus_neg_abs(z):
    return jnp.log1p(jnp.exp(-jnp.abs(z)))


def _neg_abs(x):
    bits = lax.bitcast_convert_type(x, jnp.uint32) | jnp.uint32(0x80000000)
    return lax.bitcast_convert_type(bits, F32)


def _iota(shape, dim):
    return lax.broadcasted_iota(jnp.int32, shape, dim)


def _cross_attend(x, g_cross, wq_ref, wo_ref, mkb_ref, mvb_ref):
    d = x.shape[-1]
    hd = d // X_HEADS
    hc = _rms(x, g_cross).astype(BF16)
    q = _dot(hc, wq_ref[...]) * (hd ** -0.5)
    outs = []
    for h in range(X_HEADS):
        sl = slice(h * hd, (h + 1) * hd)
        s = _dot_nt(q[:, sl].astype(BF16), mkb_ref[:, sl])
        e = jnp.exp(s - jnp.max(s, axis=-1, keepdims=True))
        l = jnp.sum(e, axis=-1, keepdims=True)
        outs.append(_dot(e.astype(BF16), mvb_ref[:, sl]) / l)
    o = jnp.concatenate(outs, axis=-1).astype(BF16)
    return x + _dot(o, wo_ref[...])


def _memkv_kernel(m_ref, g_ref, wk_ref, wv_ref, k_ref, v_ref):
    m = _rms(m_ref[0], g_ref[0]).astype(BF16)
    k_ref[0, 0] = _dot(m, wk_ref[0])
    v_ref[0, 0] = _dot(m, wv_ref[0])


def _memkv(mem, g_mem, w_ck, w_cv):
    b, n, d = mem.shape
    depth = g_mem.shape[0]
    out = jax.ShapeDtypeStruct((depth, b, n, d), F32)
    wspec = pl.BlockSpec((1, d, d), lambda l, i: (l, 0, 0))
    ospec = pl.BlockSpec((1, 1, n, d), lambda l, i: (l, i, 0, 0))
    return pl.pallas_call(
        _memkv_kernel,
        grid=(depth, b),
        in_specs=[pl.BlockSpec((1, n, d), lambda l, i: (i, 0, 0)),
                  pl.BlockSpec((1, 1, d), lambda l, i: (l, 0, 0)),
                  wspec, wspec],
        out_specs=[ospec, ospec],
        out_shape=[out, out],
        compiler_params=pltpu.CompilerParams(
            dimension_semantics=("arbitrary", "arbitrary"), vmem_limit_bytes=VMEM_LIMIT),
        name="mem_kv",
    )(mem, g_mem.reshape(depth, 1, d), w_ck, w_cv)


def _even_kernel(*refs, tq, nt_static, past, has_state):
    it = iter(refs)
    x_ref = next(it)
    pk_ref = next(it) if past else None
    pv_ref = next(it) if past else None
    s0_ref = next(it) if has_state else None
    mk_ref, mv_ref = next(it), next(it)
    gmix_ref, w_ref, wal_ref, bal_ref, ggla_ref, wout_ref = (next(it) for _ in range(6))
    gcr_ref, wq_ref, wo_ref = next(it), next(it), next(it)
    xo_ref, ko_ref, vo_ref, so_ref = next(it), next(it), next(it), next(it)
    kscr, vscr, mkb, mvb, sst, qs, acc, car = (next(it) for _ in range(8))

    t = pl.program_id(1)
    nt = pl.num_programs(1)
    hist = kscr.shape[1]
    seq = tq * nt_static

    @pl.when(t == 0)
    def _init():
        mkb[...] = mk_ref[0, 0].astype(BF16)
        mvb[...] = mv_ref[0, 0].astype(BF16)
        half0 = (_iota((1, SB_W), 1) % LANES) < SB_DIM
        for c0 in range(0, past, PAST_SLAB):
            c1 = min(c0 + PAST_SLAB, past)
            pk = pk_ref[0, 0, :, c0:c1].T
            pv = pv_ref[0, 0, :, c0:c1].T
            for e in range(2):
                keep_e = half0 if e == 0 else jnp.logical_not(half0)
                kscr[e, c0:c1, :] = jnp.where(keep_e, pk, 0.0).astype(BF16)
                vscr[e, c0:c1, :] = jnp.where(keep_e, pv, 0.0).astype(BF16)
        for e in range(2):
            if hist > past + seq:
                kscr[e, past + seq:hist, :] = jnp.zeros((hist - past - seq, SB_W), BF16)
                vscr[e, past + seq:hist, :] = jnp.zeros((hist - past - seq, SB_W), BF16)
        if has_state:
            for h in range(GLA_HEADS):
                sst[h * GLA_DK:(h + 1) * GLA_DK, :] = s0_ref[0, 0, h]
        else:
            sst[...] = jnp.zeros(sst.shape, F32)

    x = x_ref[0]
    h = _rms(x, gmix_ref[...]).astype(BF16)

    def proj(c0, c1):
        return _dot(h, w_ref[:, c0:c1])

    qa = proj(C_QA, C_KA) * (GLA_DK ** -0.5)
    ka = proj(C_KA, C_VA)
    va = proj(C_VA, C_GA)
    ga = proj(C_GA, C_QB)
    ra = proj(C_RA, C_END).astype(BF16)
    za = _dot(ra, wal_ref[...]) + bal_ref[...]
    log_a = (jnp.minimum(za, 0.0) - _softplus_neg_abs(za)) * (1.0 / GLA_TAU)

    cl = GLA_CHUNK
    nc = tq // cl
    row_t = _iota((tq, tq), 0)
    col_t = _iota((tq, tq), 1)
    causal = ((row_t // cl) == (col_t // cl)) & (col_t <= row_t)
    ltri = jnp.where(causal, 1.0, 0.0).astype(BF16)
    g_hi, g_lo = _split2(log_a)
    b = _dot(jnp.concatenate([ltri, ltri], axis=1),
             jnp.concatenate([g_hi, g_lo], axis=0))
    bt = b.T
    lane_t = _iota((1, tq), 1)
    b_last = [bt[:, (ci + 1) * cl - 1:(ci + 1) * cl] for ci in range(nc)]
    bl = jnp.broadcast_to(b_last[0], bt.shape)
    for ci in range(1, nc):
        bl = jnp.where(lane_t >= ci * cl, b_last[ci], bl)
    qe = qa * jnp.exp(b)
    ke = (ka * jnp.exp(-b)).astype(BF16)
    kdt = (ka.T * jnp.exp(bl - bt)).astype(BF16)
    vab = va.astype(BF16)
    lane_kw = _iota((1, GLA_KW), 1)
    zero_b = jnp.zeros((), BF16)
    qeh, o_intra, upd = [], [], []
    for hh in range(GLA_HEADS):
        ks = slice(hh * GLA_DK, (hh + 1) * GLA_DK)
        vs = slice(hh * GLA_DV, (hh + 1) * GLA_DV)
        q_h = jnp.where((lane_kw >= hh * GLA_DK) & (lane_kw < (hh + 1) * GLA_DK), qe, 0.0).astype(BF16)
        qeh.append(q_h)
        att = jnp.where(causal, _dot_nt(q_h, ke), 0.0).astype(BF16)
        o_intra.append(_dot(att, vab[:, vs]))
        kd_h = kdt[ks]
        kd_c = [jnp.where((lane_t >= ci * cl) & (lane_t < (ci + 1) * cl), kd_h, zero_b) for ci in range(nc)]
        upd.append(_dot(jnp.concatenate(kd_c, axis=0) if nc > 1 else kd_c[0], vab[:, vs]))
    s_all = sst[...]
    o_inter = []
    for ci in range(nc):
        rs = slice(ci * cl, (ci + 1) * cl)
        q_c = jnp.concatenate([qeh[hh][rs] for hh in range(GLA_HEADS)], axis=0)
        o_inter.append(_dot(q_c, s_all.astype(BF16)))
        u_c = jnp.concatenate([upd[hh][ci * GLA_DK:(ci + 1) * GLA_DK] for hh in range(GLA_HEADS)], axis=0)
        s_all = jnp.exp(b_last[ci]) * s_all + u_c
    sst[...] = s_all
    o_heads = []
    for hh in range(GLA_HEADS):
        inter = [o_inter[ci][hh * cl:(hh + 1) * cl] for ci in range(nc)]
        o_heads.append(o_intra[hh] + (jnp.concatenate(inter, axis=0) if nc > 1 else inter[0]))
    oa = jnp.concatenate(o_heads, axis=-1)
    ggla = ggla_ref[...]
    oa_n = []
    for hh in range(GLA_HEADS):
        vs = slice(hh * GLA_DV, (hh + 1) * GLA_DV)
        oa_n.append(_rms(oa[:, vs], ggla[:, vs]))
    oa = jnp.concatenate(oa_n, axis=-1) * _silu(ga)

    @pl.when(t == nt - 1)
    def _emit_state():
        for hh in range(GLA_HEADS):
            so_ref[0, hh] = sst[hh * GLA_DK:(hh + 1) * GLA_DK, :]

    qb = proj(C_QB, C_KB) * (SB_DIM ** -0.5 * LOG2_E)
    kb = proj(C_KB, C_VB)
    vb = proj(C_VB, C_GB)
    gb = proj(C_GB, C_RA)
    ko_ref[0] = kb
    vo_ref[0] = vb
    half = (_iota((1, SB_W), 1) % LANES) < SB_DIM
    row0 = pl.multiple_of(past + t * tq, min(tq, SB_KEYS))
    for e in range(2):
        keep_e = half if e == 0 else jnp.logical_not(half)
        kscr[e, pl.ds(row0, tq), :] = jnp.where(keep_e, kb, 0.0).astype(BF16)
        vscr[e, pl.ds(row0, tq), :] = jnp.where(keep_e, vb, 0.0).astype(BF16)
    qs[...] = qb.astype(BF16)
    acc[...] = jnp.zeros(acc.shape, F32)
    car[...] = jnp.zeros(car.shape, F32)

    ue_r = _iota((2 * SB_KEYS, 2 * LANES), 0) % SB_KEYS
    ue_c = _iota((2 * SB_KEYS, 2 * LANES), 1)
    ue2 = jnp.where((ue_c < LANES) | (ue_r > ue_c - LANES), 1.0, 0.0).astype(BF16)
    n_pairs = SB_HEADS // 2

    def sb_block(r0, rs, masked):
        nr = tq - rs
        keys = pl.ds(r0, SB_KEYS)
        if masked:
            vis1 = _iota((nr, SB_KEYS), 1) < _iota((nr, SB_KEYS), 0)
            vis2 = (_iota((nr, 2 * SB_KEYS), 1) % SB_KEYS) < _iota((nr, 2 * SB_KEYS), 0)
        log_b, lhs = [], []
        for p in range(n_pairs):
            ls_ = slice(p * LANES, (p + 1) * LANES)
            k2 = jnp.concatenate([kscr[0, keys, ls_], kscr[1, keys, ls_]], axis=0)
            z = _dot_nt(qs[rs:tq, ls_], k2)
            sp = jnp.maximum(z, 0.0) + jnp.log2(1.0 + jnp.exp2(_neg_abs(z)))
            log_b.append(z - sp)
            if masked:
                sp = jnp.where(vis2, sp, 0.0)
            hi, lo = _split2(sp)
            for e in range(2):
                es = slice(e * SB_KEYS, (e + 1) * SB_KEYS)
                lhs.append(jnp.concatenate([hi[:, es], lo[:, es]], axis=1))
        sc = _dot(jnp.concatenate(lhs, axis=0), ue2)
        for p in range(n_pairs):
            ls_ = slice(p * LANES, (p + 1) * LANES)
            ws = []
            for e in range(2):
                hh = 2 * p + e
                sc_h = sc[hh * nr:(hh + 1) * nr]
                carry = car[hh, rs:tq, :]
                w = jnp.exp2(log_b[p][:, e * SB_KEYS:(e + 1) * SB_KEYS] - sc_h[:, LANES:] - carry)
                if masked:
                    w = jnp.where(vis1, w, 0.0)
                ws.append(w.astype(BF16))
                car[hh, rs:tq, :] = carry + sc_h[:, :LANES]
            v2 = jnp.concatenate([vscr[0, keys, ls_], vscr[1, keys, ls_]], axis=0)
            acc[p, rs:tq, :] += _dot(jnp.concatenate(ws, axis=1), v2)

    for dblk in reversed(range(pl.cdiv(tq, SB_KEYS))):
        sb_block(pl.multiple_of(row0 + dblk * SB_KEYS, SB_KEYS), dblk * SB_KEYS, True)

    nblk = (past + t * tq) // SB_KEYS
    step = 2 if (past % (2 * SB_KEYS) == 0 and tq % (2 * SB_KEYS) == 0) else 1

    def more(c):
        return jnp.logical_and(c[0] < nblk, c[1] > 0)

    def below(c):
        i = c[0]
        for u in range(step):
            sb_block(pl.multiple_of((nblk - 1 - i - u) * SB_KEYS, SB_KEYS), 0, False)
        lowest = car[0]
        for hh in range(1, SB_HEADS):
            lowest = jnp.minimum(lowest, car[hh])
        live = (jnp.min(lowest) < SB_UNDERFLOW_LOG2).astype(jnp.int32)
        return i + step, live

    lax.while_loop(more, below, (jnp.int32(0), jnp.int32(1)))

    ob = jnp.concatenate([acc[p] for p in range(n_pairs)], axis=-1)
    ob = ob * _silu(gb)

    mix = jnp.concatenate([oa, ob], axis=-1).astype(BF16)
    x = x + _dot(mix, wout_ref[...])
    xo_ref[0] = _cross_attend(x, gcr_ref[...], wq_ref, wo_ref, mkb, mvb)


def _const_spec(shape):
    nd = len(shape)
    return pl.BlockSpec(shape, lambda b, t: (0,) * nd, pipeline_mode=pl.Buffered(1))


def _even_layer(x, past_k, past_v, s0, mem_k, mem_v, layer, wts, tq):
    bsz, t, d = x.shape
    past = 0 if past_k is None else past_k.shape[-1]
    has_state = s0 is not None
    nmem = mem_k.shape[2]
    li = layer // 2
    assert t % tq == 0 and tq % GLA_CHUNK == 0 and past % SB_KEYS == 0
    assert (tq % SB_KEYS == 0 or t == tq) and SB_KEYS == LANES

    tile = lambda b, i: (b, i, 0)
    args, specs = [x], [pl.BlockSpec((1, tq, d), tile)]
    if past:
        args += [past_k, past_v]
        specs += [pl.BlockSpec((1, 1, SB_W, past), lambda b, i: (li, b, 0, 0))] * 2
    if has_state:
        args.append(s0)
        specs.append(pl.BlockSpec((1, 1, GLA_HEADS, GLA_DK, GLA_DV), lambda b, i: (li, b, 0, 0, 0)))
    args += [mem_k, mem_v]
    specs += [pl.BlockSpec((1, 1, nmem, d), lambda b, i: (layer, b, 0, 0))] * 2
    for name in ("g_mix", "w_in", "w_alpha", "b_alpha", "g_gla", "w_out", "g_cross", "w_cq", "w_co"):
        args.append(wts[name])
        specs.append(_const_spec(wts[name].shape))

    out_shape = [jax.ShapeDtypeStruct((bsz, t, d), F32),
                 jax.ShapeDtypeStruct((bsz, t, SB_W), F32),
                 jax.ShapeDtypeStruct((bsz, t, SB_W), F32),
                 jax.ShapeDtypeStruct((bsz, GLA_HEADS, GLA_DK, GLA_DV), F32)]
    out_specs = [pl.BlockSpec((1, tq, d), tile),
                 pl.BlockSpec((1, tq, SB_W), tile),
                 pl.BlockSpec((1, tq, SB_W), tile),
                 pl.BlockSpec((1, GLA_HEADS, GLA_DK, GLA_DV), lambda b, i: (b, 0, 0, 0))]
    hist = past + pl.cdiv(t, SB_KEYS) * SB_KEYS
    scratch = [pltpu.VMEM((2, hist, SB_W), BF16), pltpu.VMEM((2, hist, SB_W), BF16),
               pltpu.VMEM((nmem, d), BF16), pltpu.VMEM((nmem, d), BF16),
               pltpu.VMEM((GLA_KW, GLA_DV), F32),
               pltpu.VMEM((tq, SB_W), BF16),
               pltpu.VMEM((SB_HEADS // 2, tq, LANES), F32),
               pltpu.VMEM((SB_HEADS, tq, LANES), F32)]
    return pl.pallas_call(
        functools.partial(_even_kernel, tq=tq, nt_static=t // tq, past=past, has_state=has_state),
        grid=(bsz, t // tq),
        in_specs=specs, out_specs=out_specs, out_shape=out_shape, scratch_shapes=scratch,
        compiler_params=pltpu.CompilerParams(
            dimension_semantics=("arbitrary", "arbitrary"), vmem_limit_bytes=VMEM_LIMIT),
        name="even_layer",
    )(*args)


def _odd_kernel(*refs, tq, cl, emit_vn, final):
    it = iter(refs)
    x_ref, mk_ref, mv_ref = next(it), next(it), next(it)
    gmix_ref, w_ref, gv_ref, bv_ref, wsp_ref, bsp_ref, wout_ref = (next(it) for _ in range(7))
    gcr_ref, wq_ref, wo_ref = next(it), next(it), next(it)
    gfin_ref = next(it) if final else None
    xo_ref = next(it)
    vn_ref = next(it) if emit_vn else None
    mkb, mvb = next(it), next(it)

    @pl.when(pl.program_id(1) == 0)
    def _init():
        mkb[...] = mk_ref[0, 0].astype(BF16)
        mvb[...] = mv_ref[0, 0].astype(BF16)

    x = x_ref[0]
    d = x.shape[-1]
    h = _rms(x, gmix_ref[...]).astype(BF16)
    u = _dot(h, w_ref[:, 0:d])
    v = _dot(h, w_ref[:, d:2 * d])
    g = _dot(h, w_ref[:, 2 * d:3 * d])

    gv = _gelu(v)
    xc = gv - jnp.mean(gv, axis=-1, keepdims=True)
    v_n = xc * lax.rsqrt(jnp.mean(xc * xc, axis=-1, keepdims=True) + EPS) * gv_ref[...] + bv_ref[...]
    if emit_vn:
        vn_ref[0] = v_n

    gw = d // SGU_GROUPS
    keep = _iota((cl, cl), 1) <= _iota((cl, cl), 0)
    vnb = v_n.astype(BF16)
    bsp = bsp_ref[...]
    wgs = [jnp.where(keep, wsp_ref[gi], 0.0).astype(BF16) for gi in range(SGU_GROUPS)]
    rows = []
    for c in range(tq // cl):
        cols = []
        for gi in range(SGU_GROUPS):
            cols.append(_dot(wgs[gi], vnb[c * cl:(c + 1) * cl, gi * gw:(gi + 1) * gw]) + bsp[:, gi:gi + 1])
        rows.append(jnp.concatenate(cols, axis=-1))
    s = jnp.concatenate(rows, axis=0) if len(rows) > 1 else rows[0]

    y = (_gelu(u) * s * _silu(g)).astype(BF16)
    x = x + _dot(y, wout_ref[...])
    x = _cross_attend(x, gcr_ref[...], wq_ref, wo_ref, mkb, mvb)
    if final:
        x = _rms(x, gfin_ref[...])
    xo_ref[0] = x


def _odd_layer(x, mem_k, mem_v, layer, wts, tq, emit_vn, g_final):
    bsz, t, d = x.shape
    nmem = mem_k.shape[2]
    cl = min(SGU_CHUNK, t)
    final = g_final is not None
    assert t % tq == 0 and tq % cl == 0

    tile = lambda b, i: (b, i, 0)
    args = [x, mem_k, mem_v]
    specs = [pl.BlockSpec((1, tq, d), tile)] + [pl.BlockSpec((1, 1, nmem, d), lambda b, i: (layer, b, 0, 0))] * 2
    consts = [wts["g_mix"], wts["w_in"], wts["g_sgu"], wts["b_sgu"], wts["w_sp"][:, :cl, :cl],
              wts["b_sp_t"][:cl], wts["w_out"], wts["g_cross"], wts["w_cq"], wts["w_co"]]
    if final:
        consts.append(g_final)
    for a in consts:
        args.append(a)
        specs.append(_const_spec(a.shape))

    out_shape = [jax.ShapeDtypeStruct((bsz, t, d), F32)]
    out_specs = [pl.BlockSpec((1, tq, d), tile)]
    if emit_vn:
        out_shape.append(jax.ShapeDtypeStruct((bsz, t, d), F32))
        out_specs.append(pl.BlockSpec((1, tq, d), tile))
    return pl.pallas_call(
        functools.partial(_odd_kernel, tq=tq, cl=cl, emit_vn=emit_vn, final=final),
        grid=(bsz, t // tq),
        in_specs=specs, out_specs=out_specs, out_shape=out_shape,
        scratch_shapes=[pltpu.VMEM((nmem, d), BF16), pltpu.VMEM((nmem, d), BF16)],
        compiler_params=pltpu.CompilerParams(
            dimension_semantics=("arbitrary", "arbitrary"), vmem_limit_bytes=VMEM_LIMIT),
        name="odd_layer",
    )(*args)


def _tile_rows(t):
    return min(256, t)


def _run_group(x, past_k, past_v, s0, mem_k, mem_v, even_w, odd_w, g_final, emit_vn):
    depth = mem_k.shape[0]
    bsz, t, d = x.shape
    tq = _tile_rows(t)
    mem_k = mem_k.reshape(depth, bsz, -1, d)
    mem_v = mem_v.reshape(depth, bsz, -1, d)
    if past_k is not None:
        n_even, _, past = past_k.shape[:3]
        past_k = past_k.transpose(0, 1, 3, 4, 2).reshape(n_even, bsz, SB_W, past)
        past_v = past_v.transpose(0, 1, 3, 4, 2).reshape(n_even, bsz, SB_W, past)
    sb_k, sb_v, gla_s, sgu_v = [], [], [], []
    for l in range(depth):
        if l % 2 == 0:
            x, kb, vb, s_new = _even_layer(x, past_k, past_v, s0, mem_k, mem_v, l, even_w[l], tq)
            sb_k.append(kb.reshape(bsz, t, SB_HEADS, SB_DIM))
            sb_v.append(vb.reshape(bsz, t, SB_HEADS, SB_DIM))
            gla_s.append(s_new)
        else:
            outs = _odd_layer(x, mem_k, mem_v, l, odd_w[l], tq, emit_vn,
                              g_final if l == depth - 1 else None)
            x = outs[0]
            if emit_vn:
                sgu_v.append(outs[1])
    return x, jnp.stack(sb_k), jnp.stack(sb_v), jnp.stack(gla_s), (jnp.stack(sgu_v) if emit_vn else None)


def kernel(x_prompt, x_sample, cache_sb_k, cache_sb_v, state_gla, cache_mem_k, cache_mem_v, mem_prompt, g_mix, w_in_even, w_alpha, b_alpha, g_gla_out, w_out_even, w_in_odd, g_sgu_v, b_sgu_v, w_sp, b_sp, w_out_odd, g_cross, g_mem, w_cq, w_ck, w_cv, w_co, g_final):
    depth, d = g_mix.shape
    bp, n_mem, _ = mem_prompt.shape
    row = lambda a: a.reshape(1, -1)

    even_w, odd_w = {}, {}
    for l in range(depth):
        i = l // 2
        shared = dict(g_mix=row(g_mix[l]), g_cross=row(g_cross[l]),
                      w_cq=w_cq[l].astype(BF16), w_co=w_co[l].astype(BF16))
        if l % 2 == 0:
            w = w_in_even[i]
            o_ra = C_QB
            w_main = jnp.concatenate(
                [w[:, :o_ra], w[:, o_ra + GLA_RANK:], w[:, o_ra:o_ra + GLA_RANK],
                 jnp.zeros((d, RA_PAD - GLA_RANK), w.dtype)], axis=1).astype(BF16)
            w_al = jnp.concatenate(
                [w_alpha[i], jnp.zeros((RA_PAD - GLA_RANK, GLA_KW), w_alpha.dtype)], axis=0).astype(BF16)
            even_w[l] = dict(shared, w_in=w_main, w_alpha=w_al, b_alpha=row(b_alpha[i]),
                             g_gla=row(g_gla_out[i]), w_out=w_out_even[i].astype(BF16))
        else:
            odd_w[l] = dict(shared, w_in=w_in_odd[i].astype(BF16), g_sgu=row(g_sgu_v[i]),
                            b_sgu=row(b_sgu_v[i]), w_sp=w_sp[i], b_sp_t=b_sp[i].T,
                            w_out=w_out_odd[i].astype(BF16))

    mem_k_p, mem_v_p = _memkv(mem_prompt, g_mem, w_ck.astype(BF16), w_cv.astype(BF16))
    gfin = row(g_final)

    y_p, sbk_p, sbv_p, gla_p, _ = _run_group(
        x_prompt, None, None, None, mem_k_p, mem_v_p, even_w, odd_w, gfin, False)
    y_s, sbk_s, sbv_s, gla_s, sgu_s = _run_group(
        x_sample, cache_sb_k, cache_sb_v, state_gla, cache_mem_k, cache_mem_v, even_w, odd_w, gfin, True)

    mshape = (depth, bp, n_mem, X_HEADS, d // X_HEADS)
    return (y_p, y_s, sbk_p, sbv_p, gla_p, mem_k_p.reshape(mshape), mem_v_p.reshape(mshape),
            sbk_s, sbv_s, gla_s, sgu_s)
```

```python
import functools

import jax
import jax.numpy as jnp
from jax import lax
from jax.experimental import pallas as pl
from jax.experimental.pallas import tpu as pltpu

F32 = jnp.float32
BF16 = jnp.bfloat16

EPS = 1e-6
GLA_HEADS = 4
GLA_DK = 64
GLA_DV = 128
GLA_RANK = 16
GLA_TAU = 16.0
GLA_CHUNK = 64
GLA_KW = GLA_HEADS * GLA_DK
GLA_VW = GLA_HEADS * GLA_DV
SB_HEADS = 8
SB_DIM = 64
SB_W = SB_HEADS * SB_DIM
SB_KEYS = 128
SGU_GROUPS = 4
SGU_CHUNK = 128
X_HEADS = 4
LANES = 128
RA_PAD = LANES
VMEM_LIMIT = 56 * 1024 * 1024
LOG2_E = 1.4426950408889634
SB_UNDERFLOW_LOG2 = 160.0
PAST_SLAB = 256

C_QA, C_KA, C_VA, C_GA = 0, 256, 512, 1024
C_QB, C_KB, C_VB, C_GB, C_RA, C_END = 1536, 2048, 2560, 3072, 3584, 3712


def _dot(a, b):
    return jnp.dot(a, b, preferred_element_type=F32)


def _dot_nt(a, b):
    return lax.dot_general(a, b, (((1,), (1,)), ((), ())), preferred_element_type=F32)


def _split2(a):
    hi = a.astype(BF16)
    lo = (a - hi.astype(F32)).astype(BF16)
    return hi, lo


def _rms(x, g):
    return x * lax.rsqrt(jnp.mean(x * x, axis=-1, keepdims=True) + EPS) * g


def _sigmoid(x):
    return 1.0 / (1.0 + jnp.exp(-x))


def _silu(x):
    return x * _sigmoid(x)


def _gelu(x):
    return x * (0.5 * (1.0 + jnp.tanh(0.7978845608028654 * (x + 0.044715 * (x * x * x)))))


def _softplus_neg_abs(z):
    return jnp.log1p(jnp.exp(-jnp.abs(z)))


def _iota(shape, dim):
    return lax.broadcasted_iota(jnp.int32, shape, dim)


def _cross_attend(x, g_cross, wq_ref, wo_ref, mkb_ref, mvb_ref):
    d = x.shape[-1]
    hd = d // X_HEADS
    hc = _rms(x, g_cross).astype(BF16)
    q = _dot(hc, wq_ref[...]) * (hd ** -0.5)
    outs = []
    for h in range(X_HEADS):
        sl = slice(h * hd, (h + 1) * hd)
        s = _dot_nt(q[:, sl].astype(BF16), mkb_ref[:, sl])
        e = jnp.exp(s - jnp.max(s, axis=-1, keepdims=True))
        l = jnp.sum(e, axis=-1, keepdims=True)
        outs.append(_dot(e.astype(BF16), mvb_ref[:, sl]) / l)
    o = jnp.concatenate(outs, axis=-1).astype(BF16)
    return x + _dot(o, wo_ref[...])


def _memkv_kernel(m_ref, g_ref, wk_ref, wv_ref, k_ref, v_ref, kh_ref, vh_ref):
    m = _rms(m_ref[0], g_ref[0]).astype(BF16)
    k = _dot(m, wk_ref[0])
    v = _dot(m, wv_ref[0])
    k_ref[0, 0] = k
    v_ref[0, 0] = v
    hd = k.shape[-1] // X_HEADS
    for h in range(X_HEADS):
        kh_ref[0, 0, :, h, :] = k[:, h * hd:(h + 1) * hd]
        vh_ref[0, 0, :, h, :] = v[:, h * hd:(h + 1) * hd]


def _memkv(mem, g_mem, w_ck, w_cv):
    b, n, d = mem.shape
    depth = g_mem.shape[0]
    hd = d // X_HEADS
    flat = jax.ShapeDtypeStruct((depth, b, n, d), F32)
    heads = jax.ShapeDtypeStruct((depth, b, n, X_HEADS, hd), F32)
    wspec = pl.BlockSpec((1, d, d), lambda l, i: (l, 0, 0))
    fspec = pl.BlockSpec((1, 1, n, d), lambda l, i: (l, i, 0, 0))
    hspec = pl.BlockSpec((1, 1, n, X_HEADS, hd), lambda l, i: (l, i, 0, 0, 0))
    return pl.pallas_call(
        _memkv_kernel,
        grid=(depth, b),
        in_specs=[pl.BlockSpec((1, n, d), lambda l, i: (i, 0, 0)),
                  pl.BlockSpec((1, 1, d), lambda l, i: (l, 0, 0)),
                  wspec, wspec],
        out_specs=[fspec, fspec, hspec, hspec],
        out_shape=[flat, flat, heads, heads],
        compiler_params=pltpu.CompilerParams(
            dimension_semantics=("arbitrary", "arbitrary"), vmem_limit_bytes=VMEM_LIMIT),
        name="mem_kv",
    )(mem, g_mem.reshape(depth, 1, d), w_ck, w_cv)


def _even_kernel(*refs, tq, nt_static, past, has_state, n_prev, kv_major):
    it = iter(refs)
    x_ref = next(it)
    pk_ref = next(it) if past else None
    pv_ref = next(it) if past else None
    s0_ref = next(it) if has_state else None
    pko_ref = next(it) if n_prev else None
    pvo_ref = next(it) if n_prev else None
    mk_ref, mv_ref = next(it), next(it)
    gmix_ref, w_ref, wal_ref, bal_ref, ggla_ref, wout_ref = (next(it) for _ in range(6))
    gcr_ref, wq_ref, wo_ref = next(it), next(it), next(it)
    xo_ref, ko_ref, vo_ref, so_ref = next(it), next(it), next(it), next(it)
    kscr, vscr, mkb, mvb, sst, qs, acc, car = (next(it) for _ in range(8))

    t = pl.program_id(1)
    nt = pl.num_programs(1)
    hist = kscr.shape[1]
    seq = tq * nt_static

    @pl.when(t == 0)
    def _init():
        mkb[...] = mk_ref[0, 0].astype(BF16)
        mvb[...] = mv_ref[0, 0].astype(BF16)
        half0 = (_iota((1, SB_W), 1) % LANES) < SB_DIM
        for c0 in range(0, past, PAST_SLAB):
            c1 = min(c0 + PAST_SLAB, past)
            pk = pk_ref[0, 0, :, c0:c1].T
            pv = pv_ref[0, 0, :, c0:c1].T
            for e in range(2):
                keep_e = half0 if e == 0 else jnp.logical_not(half0)
                kscr[e, c0:c1, :] = jnp.where(keep_e, pk, 0.0).astype(BF16)
                vscr[e, c0:c1, :] = jnp.where(keep_e, pv, 0.0).astype(BF16)
        for e in range(2):
            if hist > past + seq:
                kscr[e, past + seq:hist, :] = jnp.zeros((hist - past - seq, SB_W), BF16)
                vscr[e, past + seq:hist, :] = jnp.zeros((hist - past - seq, SB_W), BF16)
        if has_state:
            for h in range(GLA_HEADS):
                sst[h * GLA_DK:(h + 1) * GLA_DK, :] = s0_ref[0, 0, h]
        else:
            sst[...] = jnp.zeros(sst.shape, F32)

    x = x_ref[0]
    h = _rms(x, gmix_ref[...]).astype(BF16)

    def proj(c0, c1):
        return _dot(h, w_ref[:, c0:c1])

    qa = proj(C_QA, C_KA) * (GLA_DK ** -0.5)
    ka = proj(C_KA, C_VA)
    va = proj(C_VA, C_GA)
    ga = proj(C_GA, C_QB)
    ra = proj(C_RA, C_END).astype(BF16)
    za = _dot(ra, wal_ref[...]) + bal_ref[...]
    log_a = (jnp.minimum(za, 0.0) - _softplus_neg_abs(za)) * (1.0 / GLA_TAU)

    cl = GLA_CHUNK
    nc = tq // cl
    row_t = _iota((tq, tq), 0)
    col_t = _iota((tq, tq), 1)
    causal = ((row_t // cl) == (col_t // cl)) & (col_t <= row_t)
    ltri = jnp.where(causal, 1.0, 0.0).astype(BF16)
    g_hi, g_lo = _split2(log_a)
    b = _dot(jnp.concatenate([ltri, ltri], axis=1),
             jnp.concatenate([g_hi, g_lo], axis=0))
    bt = b.T
    lane_t = _iota((1, tq), 1)
    b_last = [bt[:, (ci + 1) * cl - 1:(ci + 1) * cl] for ci in range(nc)]
    bl = jnp.broadcast_to(b_last[0], bt.shape)
    for ci in range(1, nc):
        bl = jnp.where(lane_t >= ci * cl, b_last[ci], bl)
    qe = qa * jnp.exp(b)
    ke = (ka * jnp.exp(-b)).astype(BF16)
    kdt = (ka.T * jnp.exp(bl - bt)).astype(BF16)
    vab = va.astype(BF16)
    lane_kw = _iota((1, GLA_KW), 1)
    zero_b = jnp.zeros((), BF16)
    qeh, o_intra, upd = [], [], []
    for hh in range(GLA_HEADS):
        ks = slice(hh * GLA_DK, (hh + 1) * GLA_DK)
        vs = slice(hh * GLA_DV, (hh + 1) * GLA_DV)
        q_h = jnp.where((lane_kw >= hh * GLA_DK) & (lane_kw < (hh + 1) * GLA_DK), qe, 0.0).astype(BF16)
        qeh.append(q_h)
        att = jnp.where(causal, _dot_nt(q_h, ke), 0.0).astype(BF16)
        o_intra.append(_dot(att, vab[:, vs]))
        kd_h = kdt[ks]
        kd_c = [jnp.where((lane_t >= ci * cl) & (lane_t < (ci + 1) * cl), kd_h, zero_b) for ci in range(nc)]
        upd.append(_dot(jnp.concatenate(kd_c, axis=0) if nc > 1 else kd_c[0], vab[:, vs]))
    s_all = sst[...]
    o_inter = []
    for ci in range(nc):
        rs = slice(ci * cl, (ci + 1) * cl)
        q_c = jnp.concatenate([qeh[hh][rs] for hh in range(GLA_HEADS)], axis=0)
        o_inter.append(_dot(q_c, s_all.astype(BF16)))
        u_c = jnp.concatenate([upd[hh][ci * GLA_DK:(ci + 1) * GLA_DK] for hh in range(GLA_HEADS)], axis=0)
        s_all = jnp.exp(b_last[ci]) * s_all + u_c
    sst[...] = s_all
    o_heads = []
    for hh in range(GLA_HEADS):
        inter = [o_inter[ci][hh * cl:(hh + 1) * cl] for ci in range(nc)]
        o_heads.append(o_intra[hh] + (jnp.concatenate(inter, axis=0) if nc > 1 else inter[0]))
    oa = jnp.concatenate(o_heads, axis=-1)
    ggla = ggla_ref[...]
    oa_n = []
    for hh in range(GLA_HEADS):
        vs = slice(hh * GLA_DV, (hh + 1) * GLA_DV)
        oa_n.append(_rms(oa[:, vs], ggla[:, vs]))
    oa = jnp.concatenate(oa_n, axis=-1) * _silu(ga)

    @pl.when(t == nt - 1)
    def _emit_state():
        for hh in range(GLA_HEADS):
            so_ref[0, hh] = sst[hh * GLA_DK:(hh + 1) * GLA_DK, :]

    qb = proj(C_QB, C_KB) * (SB_DIM ** -0.5 * LOG2_E)
    kb = proj(C_KB, C_VB)
    vb = proj(C_VB, C_GB)
    gb = proj(C_GB, C_RA)
    if kv_major:
        for j in range(n_prev):
            ko_ref[j, 0] = pko_ref[j, 0]
            vo_ref[j, 0] = pvo_ref[j, 0]
        ko_ref[n_prev, 0] = kb.T
        vo_ref[n_prev, 0] = vb.T
    else:
        ko_ref[0] = kb
        vo_ref[0] = vb
    half = (_iota((1, SB_W), 1) % LANES) < SB_DIM
    row0 = pl.multiple_of(past + t * tq, min(tq, SB_KEYS))
    for e in range(2):
        keep_e = half if e == 0 else jnp.logical_not(half)
        kscr[e, pl.ds(row0, tq), :] = jnp.where(keep_e, kb, 0.0).astype(BF16)
        vscr[e, pl.ds(row0, tq), :] = jnp.where(keep_e, vb, 0.0).astype(BF16)
    qs[...] = qb.astype(BF16)
    acc[...] = jnp.zeros(acc.shape, F32)
    car[...] = jnp.zeros(car.shape, F32)

    ue_r = _iota((2 * SB_KEYS, 2 * LANES), 0) % SB_KEYS
    ue_c = _iota((2 * SB_KEYS, 2 * LANES), 1)
    ue2 = jnp.where((ue_c < LANES) | (ue_r > ue_c - LANES), 1.0, 0.0).astype(BF16)
    n_pairs = SB_HEADS // 2

    def sb_block(r0, rs, masked):
        nr = tq - rs
        keys = pl.ds(r0, SB_KEYS)
        if masked:
            vis1 = _iota((nr, SB_KEYS), 1) < _iota((nr, SB_KEYS), 0)
            vis2 = (_iota((nr, 2 * SB_KEYS), 1) % SB_KEYS) < _iota((nr, 2 * SB_KEYS), 0)
        log_b, lhs = [], []
        for p in range(n_pairs):
            ls_ = slice(p * LANES, (p + 1) * LANES)
            k2 = jnp.concatenate([kscr[0, keys, ls_], kscr[1, keys, ls_]], axis=0)
            z = _dot_nt(qs[rs:tq, ls_], k2)
            sp = jnp.maximum(z, 0.0) + jnp.log2(1.0 + jnp.exp2(-jnp.abs(z)))
            log_b.append(z - sp)
            if masked:
                sp = jnp.where(vis2, sp, 0.0)
            hi, lo = _split2(sp)
            for e in range(2):
                es = slice(e * SB_KEYS, (e + 1) * SB_KEYS)
                lhs.append(jnp.concatenate([hi[:, es], lo[:, es]], axis=1))
        sc = _dot(jnp.concatenate(lhs, axis=0), ue2)
        for p in range(n_pairs):
            ls_ = slice(p * LANES, (p + 1) * LANES)
            ws = []
            for e in range(2):
                hh = 2 * p + e
                sc_h = sc[hh * nr:(hh + 1) * nr]
                carry = car[hh, rs:tq, :]
                w = jnp.exp2(log_b[p][:, e * SB_KEYS:(e + 1) * SB_KEYS] - sc_h[:, LANES:] - carry)
                if masked:
                    w = jnp.where(vis1, w, 0.0)
                ws.append(w.astype(BF16))
                car[hh, rs:tq, :] = carry + sc_h[:, :LANES]
            v2 = jnp.concatenate([vscr[0, keys, ls_], vscr[1, keys, ls_]], axis=0)
            acc[p, rs:tq, :] += _dot(jnp.concatenate(ws, axis=1), v2)

    for dblk in reversed(range(pl.cdiv(tq, SB_KEYS))):
        sb_block(pl.multiple_of(row0 + dblk * SB_KEYS, SB_KEYS), dblk * SB_KEYS, True)

    nblk = (past + t * tq) // SB_KEYS
    step = 2 if (past % (2 * SB_KEYS) == 0 and tq % (2 * SB_KEYS) == 0) else 1

    def more(c):
        return jnp.logical_and(c[0] < nblk, c[1] > 0)

    def below(c):
        i = c[0]
        for u in range(step):
            sb_block(pl.multiple_of((nblk - 1 - i - u) * SB_KEYS, SB_KEYS), 0, False)
        lowest = car[0]
        for hh in range(1, SB_HEADS):
            lowest = jnp.minimum(lowest, car[hh])
        live = (jnp.min(lowest) < SB_UNDERFLOW_LOG2).astype(jnp.int32)
        return i + step, live

    lax.while_loop(more, below, (jnp.int32(0), jnp.int32(1)))

    ob = jnp.concatenate([acc[p] for p in range(n_pairs)], axis=-1)
    ob = ob * _silu(gb)

    mix = jnp.concatenate([oa, ob], axis=-1).astype(BF16)
    x = x + _dot(mix, wout_ref[...])
    xo_ref[0] = _cross_attend(x, gcr_ref[...], wq_ref, wo_ref, mkb, mvb)


def _const_spec(shape):
    nd = len(shape)
    return pl.BlockSpec(shape, lambda b, t: (0,) * nd, pipeline_mode=pl.Buffered(1))


def _even_layer(x, past_k, past_v, s0, prev_kv, mem_k, mem_v, layer, wts, tq):
    bsz, t, d = x.shape
    past = 0 if past_k is None else past_k.shape[-1]
    has_state = s0 is not None
    nmem = mem_k.shape[2]
    li = layer // 2
    kv_major = tq % LANES == 0
    n_prev = 0 if prev_kv is None else prev_kv[0].shape[0]
    assert kv_major or prev_kv is None
    assert t % tq == 0 and tq % GLA_CHUNK == 0 and past % SB_KEYS == 0
    assert (tq % SB_KEYS == 0 or t == tq) and SB_KEYS == LANES

    tile = lambda b, i: (b, i, 0)
    args, specs = [x], [pl.BlockSpec((1, tq, d), tile)]
    if past:
        args += [past_k, past_v]
        specs += [pl.BlockSpec((1, 1, SB_W, past), lambda b, i: (li, b, 0, 0))] * 2
    if has_state:
        args.append(s0)
        specs.append(pl.BlockSpec((1, 1, GLA_HEADS, GLA_DK, GLA_DV), lambda b, i: (li, b, 0, 0, 0)))
    if n_prev:
        args += list(prev_kv)
        specs += [pl.BlockSpec((n_prev, 1, SB_W, tq), lambda b, i: (0, b, 0, i))] * 2
    args += [mem_k, mem_v]
    specs += [pl.BlockSpec((1, 1, nmem, d), lambda b, i: (layer, b, 0, 0))] * 2
    for name in ("g_mix", "w_in", "w_alpha", "b_alpha", "g_gla", "w_out", "g_cross", "w_cq", "w_co"):
        args.append(wts[name])
        specs.append(_const_spec(wts[name].shape))

    if kv_major:
        kv_shape = jax.ShapeDtypeStruct((n_prev + 1, bsz, SB_W, t), F32)
        kv_spec = pl.BlockSpec((n_prev + 1, 1, SB_W, tq), lambda b, i: (0, b, 0, i))
    else:
        kv_shape = jax.ShapeDtypeStruct((bsz, t, SB_W), F32)
        kv_spec = pl.BlockSpec((1, tq, SB_W), tile)
    out_shape = [jax.ShapeDtypeStruct((bsz, t, d), F32), kv_shape, kv_shape,
                 jax.ShapeDtypeStruct((bsz, GLA_HEADS, GLA_DK, GLA_DV), F32)]
    out_specs = [pl.BlockSpec((1, tq, d), tile), kv_spec, kv_spec,
                 pl.BlockSpec((1, GLA_HEADS, GLA_DK, GLA_DV), lambda b, i: (b, 0, 0, 0))]
    hist = past + pl.cdiv(t, SB_KEYS) * SB_KEYS
    scratch = [pltpu.VMEM((2, hist, SB_W), BF16), pltpu.VMEM((2, hist, SB_W), BF16),
               pltpu.VMEM((nmem, d), BF16), pltpu.VMEM((nmem, d), BF16),
               pltpu.VMEM((GLA_KW, GLA_DV), F32),
               pltpu.VMEM((tq, SB_W), BF16),
               pltpu.VMEM((SB_HEADS // 2, tq, LANES), F32),
               pltpu.VMEM((SB_HEADS, tq, LANES), F32)]
    return pl.pallas_call(
        functools.partial(_even_kernel, tq=tq, nt_static=t // tq, past=past, has_state=has_state,
                          n_prev=n_prev, kv_major=kv_major),
        grid=(bsz, t // tq),
        in_specs=specs, out_specs=out_specs, out_shape=out_shape, scratch_shapes=scratch,
        compiler_params=pltpu.CompilerParams(
            dimension_semantics=("arbitrary", "arbitrary"), vmem_limit_bytes=VMEM_LIMIT),
        name="even_layer",
    )(*args)


def _odd_kernel(*refs, tq, cl, emit_vn, final):
    it = iter(refs)
    x_ref, mk_ref, mv_ref = next(it), next(it), next(it)
    gmix_ref, w_ref, gv_ref, bv_ref, wsp_ref, bsp_ref, wout_ref = (next(it) for _ in range(7))
    gcr_ref, wq_ref, wo_ref = next(it), next(it), next(it)
    gfin_ref = next(it) if final else None
    xo_ref = next(it)
    vn_ref = next(it) if emit_vn else None
    mkb, mvb = next(it), next(it)

    @pl.when(pl.program_id(1) == 0)
    def _init():
        mkb[...] = mk_ref[0, 0].astype(BF16)
        mvb[...] = mv_ref[0, 0].astype(BF16)

    x = x_ref[0]
    d = x.shape[-1]
    h = _rms(x, gmix_ref[...]).astype(BF16)
    u = _dot(h, w_ref[:, 0:d])
    v = _dot(h, w_ref[:, d:2 * d])
    g = _dot(h, w_ref[:, 2 * d:3 * d])

    gv = _gelu(v)
    xc = gv - jnp.mean(gv, axis=-1, keepdims=True)
    v_n = xc * lax.rsqrt(jnp.mean(xc * xc, axis=-1, keepdims=True) + EPS) * gv_ref[...] + bv_ref[...]
    if emit_vn:
        vn_ref[0] = v_n

    gw = d // SGU_GROUPS
    keep = _iota((cl, cl), 1) <= _iota((cl, cl), 0)
    vnb = v_n.astype(BF16)
    bsp = bsp_ref[...]
    wgs = [jnp.where(keep, wsp_ref[gi], 0.0).astype(BF16) for gi in range(SGU_GROUPS)]
    rows = []
    for c in range(tq // cl):
        cols = []
        for gi in range(SGU_GROUPS):
            cols.append(_dot(wgs[gi], vnb[c * cl:(c + 1) * cl, gi * gw:(gi + 1) * gw]) + bsp[:, gi:gi + 1])
        rows.append(jnp.concatenate(cols, axis=-1))
    s = jnp.concatenate(rows, axis=0) if len(rows) > 1 else rows[0]

    y = (_gelu(u) * s * _silu(g)).astype(BF16)
    x = x + _dot(y, wout_ref[...])
    x = _cross_attend(x, gcr_ref[...], wq_ref, wo_ref, mkb, mvb)
    if final:
        x = _rms(x, gfin_ref[...])
    xo_ref[0] = x


def _odd_layer(x, mem_k, mem_v, layer, wts, tq, emit_vn, g_final):
    bsz, t, d = x.shape
    nmem = mem_k.shape[2]
    cl = min(SGU_CHUNK, t)
    final = g_final is not None
    assert t % tq == 0 and tq % cl == 0

    tile = lambda b, i: (b, i, 0)
    args = [x, mem_k, mem_v]
    specs = [pl.BlockSpec((1, tq, d), tile)]
    specs += [pl.BlockSpec((1, 1, nmem, d), lambda b, i: (layer, b, 0, 0))] * 2
    consts = [wts["g_mix"], wts["w_in"], wts["g_sgu"], wts["b_sgu"], wts["w_sp"][:, :cl, :cl],
              wts["b_sp_t"][:cl], wts["w_out"], wts["g_cross"], wts["w_cq"], wts["w_co"]]
    if final:
        consts.append(g_final)
    for a in consts:
        args.append(a)
        specs.append(_const_spec(a.shape))

    out_shape = [jax.ShapeDtypeStruct((bsz, t, d), F32)]
    out_specs = [pl.BlockSpec((1, tq, d), tile)]
    if emit_vn:
        out_shape.append(jax.ShapeDtypeStruct((bsz, t, d), F32))
        out_specs.append(pl.BlockSpec((1, tq, d), tile))
    return pl.pallas_call(
        functools.partial(_odd_kernel, tq=tq, cl=cl, emit_vn=emit_vn, final=final),
        grid=(bsz, t // tq),
        in_specs=specs, out_specs=out_specs, out_shape=out_shape,
        scratch_shapes=[pltpu.VMEM((nmem, d), BF16), pltpu.VMEM((nmem, d), BF16)],
        compiler_params=pltpu.CompilerParams(
            dimension_semantics=("arbitrary", "arbitrary"), vmem_limit_bytes=VMEM_LIMIT),
        name="odd_layer",
    )(*args)


EVEN_TILE = 256
ODD_TILE = 512


def _tile_rows(t, layer):
    return min(EVEN_TILE if layer % 2 == 0 else ODD_TILE, t)


def _run_group(x, past_k, past_v, s0, mem_k, mem_v, even_w, odd_w, g_final, emit_vn):
    depth = mem_k.shape[0]
    bsz, t, d = x.shape
    tq = _tile_rows(t, 0)
    mem_k = mem_k.reshape(depth, bsz, -1, d)
    mem_v = mem_v.reshape(depth, bsz, -1, d)
    if past_k is not None:
        n_even, _, past = past_k.shape[:3]
        past_k = past_k.transpose(0, 1, 3, 4, 2).reshape(n_even, bsz, SB_W, past)
        past_v = past_v.transpose(0, 1, 3, 4, 2).reshape(n_even, bsz, SB_W, past)
    kv_major = tq % LANES == 0
    sb_k, sb_v, gla_s, sgu_v = [], [], [], []
    prev_kv = None
    for l in range(depth):
        if l % 2 == 0:
            x, kb, vb, s_new = _even_layer(x, past_k, past_v, s0, prev_kv, mem_k, mem_v, l, even_w[l], tq)
            if kv_major:
                prev_kv = (kb, vb)
            else:
                sb_k.append(kb.reshape(bsz, t, SB_HEADS, SB_DIM))
                sb_v.append(vb.reshape(bsz, t, SB_HEADS, SB_DIM))
            gla_s.append(s_new)
        else:
            outs = _odd_layer(x, mem_k, mem_v, l, odd_w[l], _tile_rows(t, l), emit_vn,
                              g_final if l == depth - 1 else None)
            x = outs[0]
            if emit_vn:
                sgu_v.append(outs[1])
    if kv_major:
        sb_k, sb_v = (a.reshape(a.shape[0], bsz, SB_HEADS, SB_DIM, t).transpose(0, 1, 4, 2, 3) for a in prev_kv)
    else:
        sb_k, sb_v = jnp.stack(sb_k), jnp.stack(sb_v)
    return x, sb_k, sb_v, jnp.stack(gla_s), (jnp.stack(sgu_v) if emit_vn else None)


def kernel(x_prompt, x_sample, cache_sb_k, cache_sb_v, state_gla, cache_mem_k, cache_mem_v, mem_prompt, g_mix, w_in_even, w_alpha, b_alpha, g_gla_out, w_out_even, w_in_odd, g_sgu_v, b_sgu_v, w_sp, b_sp, w_out_odd, g_cross, g_mem, w_cq, w_ck, w_cv, w_co, g_final):
    depth, d = g_mix.shape
    bp, n_mem, _ = mem_prompt.shape
    row = lambda a: a.reshape(1, -1)

    even_w, odd_w = {}, {}
    for l in range(depth):
        i = l // 2
        shared = dict(g_mix=row(g_mix[l]), g_cross=row(g_cross[l]),
                      w_cq=w_cq[l].astype(BF16), w_co=w_co[l].astype(BF16))
        if l % 2 == 0:
            w = w_in_even[i]
            o_ra = C_QB
            w_main = jnp.concatenate(
                [w[:, :o_ra], w[:, o_ra + GLA_RANK:], w[:, o_ra:o_ra + GLA_RANK],
                 jnp.zeros((d, RA_PAD - GLA_RANK), w.dtype)], axis=1).astype(BF16)
            w_al = jnp.concatenate(
                [w_alpha[i], jnp.zeros((RA_PAD - GLA_RANK, GLA_KW), w_alpha.dtype)], axis=0).astype(BF16)
            even_w[l] = dict(shared, w_in=w_main, w_alpha=w_al, b_alpha=row(b_alpha[i]),
                             g_gla=row(g_gla_out[i]), w_out=w_out_even[i].astype(BF16))
        else:
            odd_w[l] = dict(shared, w_in=w_in_odd[i].astype(BF16), g_sgu=row(g_sgu_v[i]),
                            b_sgu=row(b_sgu_v[i]), w_sp=w_sp[i], b_sp_t=b_sp[i].T,
                            w_out=w_out_odd[i].astype(BF16))

    mem_k_p, mem_v_p, mem_k_out, mem_v_out = _memkv(mem_prompt, g_mem, w_ck.astype(BF16), w_cv.astype(BF16))
    gfin = row(g_final)

    y_p, sbk_p, sbv_p, gla_p, _ = _run_group(
        x_prompt, None, None, None, mem_k_p, mem_v_p, even_w, odd_w, gfin, False)
    y_s, sbk_s, sbv_s, gla_s, sgu_s = _run_group(
        x_sample, cache_sb_k, cache_sb_v, state_gla, cache_mem_k, cache_mem_v, even_w, odd_w, gfin, True)

    return (y_p, y_s, sbk_p, sbv_p, gla_p, mem_k_out, mem_v_out, sbk_s, sbv_s, gla_s, sgu_s)
```

```python
import functools

import jax
import jax.numpy as jnp
from jax import lax
from jax.experimental import pallas as pl
from jax.experimental.pallas import tpu as pltpu

F32 = jnp.float32
BF16 = jnp.bfloat16

EPS = 1e-6
GLA_HEADS = 4
GLA_DK = 64
GLA_DV = 128
GLA_RANK = 16
GLA_TAU = 16.0
GLA_CHUNK = 64
GLA_KW = GLA_HEADS * GLA_DK
GLA_VW = GLA_HEADS * GLA_DV
SB_HEADS = 8
SB_DIM = 64
SB_W = SB_HEADS * SB_DIM
SB_KEYS = 128
SGU_GROUPS = 4
SGU_CHUNK = 128
X_HEADS = 4
LANES = 128
RA_PAD = LANES
VMEM_LIMIT = 56 * 1024 * 1024
LOG2_E = 1.4426950408889634
CARRY = object()
SB_UNDERFLOW_LOG2 = 160.0
PAST_SLAB = 256

C_QA, C_KA, C_VA, C_GA = 0, 256, 512, 1024
C_QB, C_KB, C_VB, C_GB, C_RA, C_END = 1536, 2048, 2560, 3072, 3584, 3712


def _dot(a, b):
    return jnp.dot(a, b, preferred_element_type=F32)


def _dot_nt(a, b):
    return lax.dot_general(a, b, (((1,), (1,)), ((), ())), preferred_element_type=F32)


def _split2(a):
    hi = a.astype(BF16)
    lo = (a - hi.astype(F32)).astype(BF16)
    return hi, lo


def _rms(x, g):
    return x * lax.rsqrt(jnp.mean(x * x, axis=-1, keepdims=True) + EPS) * g


def _sigmoid(x):
    return 1.0 / (1.0 + jnp.exp(-x))


def _silu(x):
    return x * _sigmoid(x)


def _gelu(x):
    return x * (0.5 * (1.0 + jnp.tanh(0.7978845608028654 * (x + 0.044715 * (x * x * x)))))


def _softplus_neg_abs(z):
    return jnp.log1p(jnp.exp(-jnp.abs(z)))


def _iota(shape, dim):
    return lax.broadcasted_iota(jnp.int32, shape, dim)


def _cross_attend(x, g_cross, wq_ref, wo_ref, mkb_ref, mvb_ref):
    d = x.shape[-1]
    hd = d // X_HEADS
    hc = _rms(x, g_cross).astype(BF16)
    q = _dot(hc, wq_ref[...]) * (hd ** -0.5)
    outs = []
    for h in range(X_HEADS):
        sl = slice(h * hd, (h + 1) * hd)
        s = _dot_nt(q[:, sl].astype(BF16), mkb_ref[:, sl])
        e = jnp.exp(s - jnp.max(s, axis=-1, keepdims=True))
        l = jnp.sum(e, axis=-1, keepdims=True)
        outs.append(_dot(e.astype(BF16), mvb_ref[:, sl]) / l)
    o = jnp.concatenate(outs, axis=-1).astype(BF16)
    return x + _dot(o, wo_ref[...])


def _memkv_kernel(m_ref, g_ref, wk_ref, wv_ref, k_ref, v_ref, kh_ref, vh_ref):
    m = _rms(m_ref[0], g_ref[0]).astype(BF16)
    k = _dot(m, wk_ref[0])
    v = _dot(m, wv_ref[0])
    k_ref[0, 0] = k
    v_ref[0, 0] = v
    hd = k.shape[-1] // X_HEADS
    for h in range(X_HEADS):
        kh_ref[0, 0, :, h, :] = k[:, h * hd:(h + 1) * hd]
        vh_ref[0, 0, :, h, :] = v[:, h * hd:(h + 1) * hd]


def _memkv(mem, g_mem, w_ck, w_cv):
    b, n, d = mem.shape
    depth = g_mem.shape[0]
    hd = d // X_HEADS
    flat = jax.ShapeDtypeStruct((depth, b, n, d), F32)
    heads = jax.ShapeDtypeStruct((depth, b, n, X_HEADS, hd), F32)
    wspec = pl.BlockSpec((1, d, d), lambda l, i: (l, 0, 0))
    fspec = pl.BlockSpec((1, 1, n, d), lambda l, i: (l, i, 0, 0))
    hspec = pl.BlockSpec((1, 1, n, X_HEADS, hd), lambda l, i: (l, i, 0, 0, 0))
    return pl.pallas_call(
        _memkv_kernel,
        grid=(depth, b),
        in_specs=[pl.BlockSpec((1, n, d), lambda l, i: (i, 0, 0)),
                  pl.BlockSpec((1, 1, d), lambda l, i: (l, 0, 0)),
                  wspec, wspec],
        out_specs=[fspec, fspec, hspec, hspec],
        out_shape=[flat, flat, heads, heads],
        compiler_params=pltpu.CompilerParams(
            dimension_semantics=("arbitrary", "arbitrary"), vmem_limit_bytes=VMEM_LIMIT),
        name="mem_kv",
    )(mem, g_mem.reshape(depth, 1, d), w_ck, w_cv)


def _even_kernel(*refs, tq, nt_static, past, has_state, n_prev, kv_major):
    it = iter(refs)
    x_ref = next(it)
    pk_ref = next(it) if past else None
    pv_ref = next(it) if past else None
    s0_ref = next(it) if has_state else None
    pko_ref = next(it) if n_prev else None
    pvo_ref = next(it) if n_prev else None
    mk_ref, mv_ref = next(it), next(it)
    gmix_ref, w_ref, wal_ref, bal_ref, ggla_ref, wout_ref = (next(it) for _ in range(6))
    gcr_ref, wq_ref, wo_ref = next(it), next(it), next(it)
    xo_ref, ko_ref, vo_ref, so_ref = next(it), next(it), next(it), next(it)
    kscr, vscr, mkb, mvb, sst, qs, acc, car = (next(it) for _ in range(8))

    t = pl.program_id(1)
    nt = pl.num_programs(1)
    hist = kscr.shape[1]
    seq = tq * nt_static

    @pl.when(t == 0)
    def _init():
        mkb[...] = mk_ref[0, 0].astype(BF16)
        mvb[...] = mv_ref[0, 0].astype(BF16)
        half0 = (_iota((1, SB_W), 1) % LANES) < SB_DIM
        for c0 in range(0, past, PAST_SLAB):
            c1 = min(c0 + PAST_SLAB, past)
            pk = pk_ref[0, 0, :, c0:c1].T
            pv = pv_ref[0, 0, :, c0:c1].T
            for e in range(2):
                keep_e = half0 if e == 0 else jnp.logical_not(half0)
                kscr[e, c0:c1, :] = jnp.where(keep_e, pk, 0.0).astype(BF16)
                vscr[e, c0:c1, :] = jnp.where(keep_e, pv, 0.0).astype(BF16)
        for e in range(2):
            if hist > past + seq:
                kscr[e, past + seq:hist, :] = jnp.zeros((hist - past - seq, SB_W), BF16)
                vscr[e, past + seq:hist, :] = jnp.zeros((hist - past - seq, SB_W), BF16)
        if has_state:
            for h in range(GLA_HEADS):
                sst[h * GLA_DK:(h + 1) * GLA_DK, :] = s0_ref[0, 0, h]
        else:
            sst[...] = jnp.zeros(sst.shape, F32)

    x = x_ref[0]
    h = _rms(x, gmix_ref[...]).astype(BF16)

    def proj(c0, c1):
        return _dot(h, w_ref[:, c0:c1])

    qa = proj(C_QA, C_KA) * (GLA_DK ** -0.5)
    ka = proj(C_KA, C_VA)
    va = proj(C_VA, C_GA)
    ga = proj(C_GA, C_QB)
    ra = proj(C_RA, C_END).astype(BF16)
    qb = proj(C_QB, C_KB) * (SB_DIM ** -0.5 * LOG2_E)
    kb = proj(C_KB, C_VB)
    vb = proj(C_VB, C_GB)
    gb = proj(C_GB, C_RA)
    gla_out = []

    def gla_steps():
        za = _dot(ra, wal_ref[...]) + bal_ref[...]
        log_a = (jnp.minimum(za, 0.0) - _softplus_neg_abs(za)) * (1.0 / GLA_TAU)

        cl = GLA_CHUNK
        nc = tq // cl
        row_t = _iota((tq, tq), 0)
        col_t = _iota((tq, tq), 1)
        causal = ((row_t // cl) == (col_t // cl)) & (col_t <= row_t)
        ltri = jnp.where(causal, 1.0, 0.0).astype(BF16)
        g_hi, g_lo = _split2(log_a)
        b = _dot(jnp.concatenate([ltri, ltri], axis=1),
                 jnp.concatenate([g_hi, g_lo], axis=0))
        yield
        bt = b.T
        lane_t = _iota((1, tq), 1)
        b_last = [bt[:, (ci + 1) * cl - 1:(ci + 1) * cl] for ci in range(nc)]
        bl = jnp.broadcast_to(b_last[0], bt.shape)
        for ci in range(1, nc):
            bl = jnp.where(lane_t >= ci * cl, b_last[ci], bl)
        qe = qa * jnp.exp(b)
        ke = (ka * jnp.exp(-b)).astype(BF16)
        kdt = (ka.T * jnp.exp(bl - bt)).astype(BF16)
        vab = va.astype(BF16)
        yield
        lane_kw = _iota((1, GLA_KW), 1)
        zero_b = jnp.zeros((), BF16)
        qeh, o_intra, upd = [], [], []
        for hh in range(GLA_HEADS):
            ks = slice(hh * GLA_DK, (hh + 1) * GLA_DK)
            vs = slice(hh * GLA_DV, (hh + 1) * GLA_DV)
            q_h = jnp.where((lane_kw >= hh * GLA_DK) & (lane_kw < (hh + 1) * GLA_DK), qe, 0.0).astype(BF16)
            qeh.append(q_h)
            att = jnp.where(causal, _dot_nt(q_h, ke), 0.0).astype(BF16)
            o_intra.append(_dot(att, vab[:, vs]))
            yield
            kd_h = kdt[ks]
            kd_c = [jnp.where((lane_t >= ci * cl) & (lane_t < (ci + 1) * cl), kd_h, zero_b) for ci in range(nc)]
            upd.append(_dot(jnp.concatenate(kd_c, axis=0) if nc > 1 else kd_c[0], vab[:, vs]))
            yield
        s_all = sst[...]
        o_inter = []
        for ci in range(nc):
            rs = slice(ci * cl, (ci + 1) * cl)
            q_c = jnp.concatenate([qeh[hh][rs] for hh in range(GLA_HEADS)], axis=0)
            o_inter.append(_dot(q_c, s_all.astype(BF16)))
            u_c = jnp.concatenate([upd[hh][ci * GLA_DK:(ci + 1) * GLA_DK] for hh in range(GLA_HEADS)], axis=0)
            s_all = jnp.exp(b_last[ci]) * s_all + u_c
            yield
        sst[...] = s_all
        o_heads = []
        for hh in range(GLA_HEADS):
            inter = [o_inter[ci][hh * cl:(hh + 1) * cl] for ci in range(nc)]
            o_heads.append(o_intra[hh] + (jnp.concatenate(inter, axis=0) if nc > 1 else inter[0]))
        oa = jnp.concatenate(o_heads, axis=-1)
        ggla = ggla_ref[...]
        oa_n = []
        for hh in range(GLA_HEADS):
            vs = slice(hh * GLA_DV, (hh + 1) * GLA_DV)
            oa_n.append(_rms(oa[:, vs], ggla[:, vs]))
        oa = jnp.concatenate(oa_n, axis=-1) * _silu(ga)
        gla_out.append(oa)


    if kv_major:
        for j in range(n_prev):
            ko_ref[j, 0] = pko_ref[j, 0]
            vo_ref[j, 0] = pvo_ref[j, 0]
        ko_ref[n_prev, 0] = kb.T
        vo_ref[n_prev, 0] = vb.T
    else:
        ko_ref[0] = kb
        vo_ref[0] = vb
    half = (_iota((1, SB_W), 1) % LANES) < SB_DIM
    row0 = pl.multiple_of(past + t * tq, min(tq, SB_KEYS))
    for e in range(2):
        keep_e = half if e == 0 else jnp.logical_not(half)
        kscr[e, pl.ds(row0, tq), :] = jnp.where(keep_e, kb, 0.0).astype(BF16)
        vscr[e, pl.ds(row0, tq), :] = jnp.where(keep_e, vb, 0.0).astype(BF16)
    qs[...] = qb.astype(BF16)
    acc[...] = jnp.zeros(acc.shape, F32)
    car[...] = jnp.zeros(car.shape, F32)

    ue_r = _iota((2 * SB_KEYS, 2 * LANES), 0) % SB_KEYS
    ue_c = _iota((2 * SB_KEYS, 2 * LANES), 1)
    ue2 = jnp.where((ue_c < LANES) | (ue_r > ue_c - LANES), 1.0, 0.0).astype(BF16)
    n_pairs = SB_HEADS // 2

    def sb_block_steps(r0, rs, masked, gate=None):
        nr = tq - rs
        keys = pl.ds(r0, SB_KEYS)
        if masked:
            vis1 = _iota((nr, SB_KEYS), 1) < _iota((nr, SB_KEYS), 0)
            vis2 = (_iota((nr, 2 * SB_KEYS), 1) % SB_KEYS) < _iota((nr, 2 * SB_KEYS), 0)
        log_b, lhs = [], []
        for p in range(n_pairs):
            ls_ = slice(p * LANES, (p + 1) * LANES)
            k2 = jnp.concatenate([kscr[0, keys, ls_], kscr[1, keys, ls_]], axis=0)
            z = _dot_nt(qs[rs:tq, ls_], k2)
            sp = jnp.maximum(z, 0.0) + jnp.log2(1.0 + jnp.exp2(-jnp.abs(z)))
            log_b.append(z - sp)
            if masked:
                sp = jnp.where(vis2, sp, 0.0)
            hi, lo = _split2(sp)
            for e in range(2):
                es = slice(e * SB_KEYS, (e + 1) * SB_KEYS)
                lhs.append(jnp.concatenate([hi[:, es], lo[:, es]], axis=1))
            yield
        sc = _dot(jnp.concatenate(lhs, axis=0), ue2)
        yield CARRY
        for p in range(n_pairs):
            ls_ = slice(p * LANES, (p + 1) * LANES)
            ws = []
            for e in range(2):
                hh = 2 * p + e
                sc_h = sc[hh * nr:(hh + 1) * nr]
                carry = car[hh, rs:tq, :]
                w = jnp.exp2(log_b[p][:, e * SB_KEYS:(e + 1) * SB_KEYS] - sc_h[:, LANES:] - carry)
                if masked:
                    w = jnp.where(vis1, w, 0.0)
                ws.append(w.astype(BF16))
                car[hh, rs:tq, :] = carry + sc_h[:, :LANES]
            v2 = jnp.concatenate([vscr[0, keys, ls_], vscr[1, keys, ls_]], axis=0)
            if gate is not None:
                v2 = v2 * gate
            acc[p, rs:tq, :] += _dot(jnp.concatenate(ws, axis=1), v2)
            yield

    def pipelined(blocks):
        def head(g):
            for tok in g:
                if tok is CARRY:
                    return
                yield
        blocks = list(blocks)
        yield from head(blocks[0])
        for k, g in enumerate(blocks):
            if k + 1 < len(blocks):
                yield from head(blocks[k + 1])
            yield from g

    def interleave(streams):
        streams = list(streams)
        while streams:
            for g in list(streams):
                try:
                    next(g)
                except StopIteration:
                    streams.remove(g)

    nblk = (past + t * tq) // SB_KEYS
    step = 2 if (past % (2 * SB_KEYS) == 0 and tq % (2 * SB_KEYS) == 0) else 1

    def below_block(j, gate=None):
        return sb_block_steps(pl.multiple_of(j * SB_KEYS, SB_KEYS), 0, False, gate)

    gate = None if past >= step * SB_KEYS else (nblk >= step).astype(BF16)
    diag = [sb_block_steps(pl.multiple_of(row0 + dblk * SB_KEYS, SB_KEYS), dblk * SB_KEYS, True)
            for dblk in reversed(range(pl.cdiv(tq, SB_KEYS)))]
    first = [below_block(jnp.maximum(nblk - 1 - u, 0), gate) for u in range(step)]
    interleave([gla_steps(), pipelined(diag + first)])
    oa = gla_out[0]

    @pl.when(t == nt - 1)
    def _emit_state():
        for hh in range(GLA_HEADS):
            so_ref[0, hh] = sst[hh * GLA_DK:(hh + 1) * GLA_DK, :]

    def still_live():
        lowest = car[0]
        for hh in range(1, SB_HEADS):
            lowest = jnp.minimum(lowest, car[hh])
        return (jnp.min(lowest) < SB_UNDERFLOW_LOG2).astype(jnp.int32)

    def more(c):
        return jnp.logical_and(c[0] < nblk, c[1] > 0)

    def below(c):
        i = c[0]
        interleave([pipelined([below_block(nblk - 1 - i - u) for u in range(step)])])
        return i + step, still_live()

    lax.while_loop(more, below, (jnp.int32(step), still_live()))

    ob = jnp.concatenate([acc[p] for p in range(n_pairs)], axis=-1)
    ob = ob * _silu(gb)

    mix = jnp.concatenate([oa, ob], axis=-1).astype(BF16)
    x = x + _dot(mix, wout_ref[...])
    xo_ref[0] = _cross_attend(x, gcr_ref[...], wq_ref, wo_ref, mkb, mvb)


def _const_spec(shape):
    nd = len(shape)
    return pl.BlockSpec(shape, lambda b, t: (0,) * nd, pipeline_mode=pl.Buffered(1))


def _even_layer(x, past_k, past_v, s0, prev_kv, mem_k, mem_v, layer, wts, tq):
    bsz, t, d = x.shape
    past = 0 if past_k is None else past_k.shape[-1]
    has_state = s0 is not None
    nmem = mem_k.shape[2]
    li = layer // 2
    kv_major = tq % LANES == 0
    n_prev = 0 if prev_kv is None else prev_kv[0].shape[0]
    assert kv_major or prev_kv is None
    assert t % tq == 0 and tq % GLA_CHUNK == 0 and past % SB_KEYS == 0
    assert (tq % SB_KEYS == 0 or t == tq) and SB_KEYS == LANES

    tile = lambda b, i: (b, i, 0)
    args, specs = [x], [pl.BlockSpec((1, tq, d), tile)]
    if past:
        args += [past_k, past_v]
        specs += [pl.BlockSpec((1, 1, SB_W, past), lambda b, i: (li, b, 0, 0))] * 2
    if has_state:
        args.append(s0)
        specs.append(pl.BlockSpec((1, 1, GLA_HEADS, GLA_DK, GLA_DV), lambda b, i: (li, b, 0, 0, 0)))
    if n_prev:
        args += list(prev_kv)
        specs += [pl.BlockSpec((n_prev, 1, SB_W, tq), lambda b, i: (0, b, 0, i))] * 2
    args += [mem_k, mem_v]
    specs += [pl.BlockSpec((1, 1, nmem, d), lambda b, i: (layer, b, 0, 0))] * 2
    for name in ("g_mix", "w_in", "w_alpha", "b_alpha", "g_gla", "w_out", "g_cross", "w_cq", "w_co"):
        args.append(wts[name])
        specs.append(_const_spec(wts[name].shape))

    if kv_major:
        kv_shape = jax.ShapeDtypeStruct((n_prev + 1, bsz, SB_W, t), F32)
        kv_spec = pl.BlockSpec((n_prev + 1, 1, SB_W, tq), lambda b, i: (0, b, 0, i))
    else:
        kv_shape = jax.ShapeDtypeStruct((bsz, t, SB_W), F32)
        kv_spec = pl.BlockSpec((1, tq, SB_W), tile)
    out_shape = [jax.ShapeDtypeStruct((bsz, t, d), F32), kv_shape, kv_shape,
                 jax.ShapeDtypeStruct((bsz, GLA_HEADS, GLA_DK, GLA_DV), F32)]
    out_specs = [pl.BlockSpec((1, tq, d), tile), kv_spec, kv_spec,
                 pl.BlockSpec((1, GLA_HEADS, GLA_DK, GLA_DV), lambda b, i: (b, 0, 0, 0))]
    hist = past + pl.cdiv(t, SB_KEYS) * SB_KEYS
    scratch = [pltpu.VMEM((2, hist, SB_W), BF16), pltpu.VMEM((2, hist, SB_W), BF16),
               pltpu.VMEM((nmem, d), BF16), pltpu.VMEM((nmem, d), BF16),
               pltpu.VMEM((GLA_KW, GLA_DV), F32),
               pltpu.VMEM((tq, SB_W), BF16),
               pltpu.VMEM((SB_HEADS // 2, tq, LANES), F32),
               pltpu.VMEM((SB_HEADS, tq, LANES), F32)]
    return pl.pallas_call(
        functools.partial(_even_kernel, tq=tq, nt_static=t // tq, past=past, has_state=has_state,
                          n_prev=n_prev, kv_major=kv_major),
        grid=(bsz, t // tq),
        in_specs=specs, out_specs=out_specs, out_shape=out_shape, scratch_shapes=scratch,
        compiler_params=pltpu.CompilerParams(
            dimension_semantics=("arbitrary", "arbitrary"), vmem_limit_bytes=VMEM_LIMIT),
        name="even_layer",
    )(*args)


def _odd_kernel(*refs, tq, cl, emit_vn, final):
    it = iter(refs)
    x_ref, mk_ref, mv_ref = next(it), next(it), next(it)
    gmix_ref, w_ref, gv_ref, bv_ref, wsp_ref, bsp_ref, wout_ref = (next(it) for _ in range(7))
    gcr_ref, wq_ref, wo_ref = next(it), next(it), next(it)
    gfin_ref = next(it) if final else None
    xo_ref = next(it)
    vn_ref = next(it) if emit_vn else None
    mkb, mvb = next(it), next(it)

    @pl.when(pl.program_id(1) == 0)
    def _init():
        mkb[...] = mk_ref[0, 0].astype(BF16)
        mvb[...] = mv_ref[0, 0].astype(BF16)

    x = x_ref[0]
    d = x.shape[-1]
    h = _rms(x, gmix_ref[...]).astype(BF16)
    u = _dot(h, w_ref[:, 0:d])
    v = _dot(h, w_ref[:, d:2 * d])
    g = _dot(h, w_ref[:, 2 * d:3 * d])

    gv = _gelu(v)
    xc = gv - jnp.mean(gv, axis=-1, keepdims=True)
    v_n = xc * lax.rsqrt(jnp.mean(xc * xc, axis=-1, keepdims=True) + EPS) * gv_ref[...] + bv_ref[...]
    if emit_vn:
        vn_ref[0] = v_n

    gw = d // SGU_GROUPS
    keep = _iota((cl, cl), 1) <= _iota((cl, cl), 0)
    vnb = v_n.astype(BF16)
    bsp = bsp_ref[...]
    wgs = [jnp.where(keep, wsp_ref[gi], 0.0).astype(BF16) for gi in range(SGU_GROUPS)]
    rows = []
    for c in range(tq // cl):
        cols = []
        for gi in range(SGU_GROUPS):
            cols.append(_dot(wgs[gi], vnb[c * cl:(c + 1) * cl, gi * gw:(gi + 1) * gw]) + bsp[:, gi:gi + 1])
        rows.append(jnp.concatenate(cols, axis=-1))
    s = jnp.concatenate(rows, axis=0) if len(rows) > 1 else rows[0]

    y = (_gelu(u) * s * _silu(g)).astype(BF16)
    x = x + _dot(y, wout_ref[...])
    x = _cross_attend(x, gcr_ref[...], wq_ref, wo_ref, mkb, mvb)
    if final:
        x = _rms(x, gfin_ref[...])
    xo_ref[0] = x


def _odd_layer(x, mem_k, mem_v, layer, wts, tq, emit_vn, g_final):
    bsz, t, d = x.shape
    nmem = mem_k.shape[2]
    cl = min(SGU_CHUNK, t)
    final = g_final is not None
    assert t % tq == 0 and tq % cl == 0

    tile = lambda b, i: (b, i, 0)
    args = [x, mem_k, mem_v]
    specs = [pl.BlockSpec((1, tq, d), tile)]
    specs += [pl.BlockSpec((1, 1, nmem, d), lambda b, i: (layer, b, 0, 0))] * 2
    consts = [wts["g_mix"], wts["w_in"], wts["g_sgu"], wts["b_sgu"], wts["w_sp"][:, :cl, :cl],
              wts["b_sp_t"][:cl], wts["w_out"], wts["g_cross"], wts["w_cq"], wts["w_co"]]
    if final:
        consts.append(g_final)
    for a in consts:
        args.append(a)
        specs.append(_const_spec(a.shape))

    out_shape = [jax.ShapeDtypeStruct((bsz, t, d), F32)]
    out_specs = [pl.BlockSpec((1, tq, d), tile)]
    if emit_vn:
        out_shape.append(jax.ShapeDtypeStruct((bsz, t, d), F32))
        out_specs.append(pl.BlockSpec((1, tq, d), tile))
    return pl.pallas_call(
        functools.partial(_odd_kernel, tq=tq, cl=cl, emit_vn=emit_vn, final=final),
        grid=(bsz, t // tq),
        in_specs=specs, out_specs=out_specs, out_shape=out_shape,
        scratch_shapes=[pltpu.VMEM((nmem, d), BF16), pltpu.VMEM((nmem, d), BF16)],
        compiler_params=pltpu.CompilerParams(
            dimension_semantics=("arbitrary", "arbitrary"), vmem_limit_bytes=VMEM_LIMIT),
        name="odd_layer",
    )(*args)


EVEN_TILE = 256
ODD_TILE = 512


def _tile_rows(t, layer):
    return min(EVEN_TILE if layer % 2 == 0 else ODD_TILE, t)


def _run_group(x, past_k, past_v, s0, mem_k, mem_v, even_w, odd_w, g_final, emit_vn):
    depth = mem_k.shape[0]
    bsz, t, d = x.shape
    tq = _tile_rows(t, 0)
    mem_k = mem_k.reshape(depth, bsz, -1, d)
    mem_v = mem_v.reshape(depth, bsz, -1, d)
    if past_k is not None:
        n_even, _, past = past_k.shape[:3]
        past_k = past_k.transpose(0, 1, 3, 4, 2).reshape(n_even, bsz, SB_W, past)
        past_v = past_v.transpose(0, 1, 3, 4, 2).reshape(n_even, bsz, SB_W, past)
    kv_major = tq % LANES == 0
    sb_k, sb_v, gla_s, sgu_v = [], [], [], []
    prev_kv = None
    for l in range(depth):
        if l % 2 == 0:
            x, kb, vb, s_new = _even_layer(x, past_k, past_v, s0, prev_kv, mem_k, mem_v, l, even_w[l], tq)
            if kv_major:
                prev_kv = (kb, vb)
            else:
                sb_k.append(kb.reshape(bsz, t, SB_HEADS, SB_DIM))
                sb_v.append(vb.reshape(bsz, t, SB_HEADS, SB_DIM))
            gla_s.append(s_new)
        else:
            outs = _odd_layer(x, mem_k, mem_v, l, odd_w[l], _tile_rows(t, l), emit_vn,
                              g_final if l == depth - 1 else None)
            x = outs[0]
            if emit_vn:
                sgu_v.append(outs[1])
    if kv_major:
        sb_k, sb_v = (a.reshape(a.shape[0], bsz, SB_HEADS, SB_DIM, t).transpose(0, 1, 4, 2, 3) for a in prev_kv)
    else:
        sb_k, sb_v = jnp.stack(sb_k), jnp.stack(sb_v)
    return x, sb_k, sb_v, jnp.stack(gla_s), (jnp.stack(sgu_v) if emit_vn else None)


def kernel(x_prompt, x_sample, cache_sb_k, cache_sb_v, state_gla, cache_mem_k, cache_mem_v, mem_prompt, g_mix, w_in_even, w_alpha, b_alpha, g_gla_out, w_out_even, w_in_odd, g_sgu_v, b_sgu_v, w_sp, b_sp, w_out_odd, g_cross, g_mem, w_cq, w_ck, w_cv, w_co, g_final):
    depth, d = g_mix.shape
    row = lambda a: a.reshape(1, -1)

    even_w, odd_w = {}, {}
    for l in range(depth):
        i = l // 2
        shared = dict(g_mix=row(g_mix[l]), g_cross=row(g_cross[l]),
                      w_cq=w_cq[l].astype(BF16), w_co=w_co[l].astype(BF16))
        if l % 2 == 0:
            w = w_in_even[i]
            o_ra = C_QB
            w_main = jnp.concatenate(
                [w[:, :o_ra], w[:, o_ra + GLA_RANK:], w[:, o_ra:o_ra + GLA_RANK],
                 jnp.zeros((d, RA_PAD - GLA_RANK), w.dtype)], axis=1).astype(BF16)
            w_al = jnp.concatenate(
                [w_alpha[i], jnp.zeros((RA_PAD - GLA_RANK, GLA_KW), w_alpha.dtype)], axis=0).astype(BF16)
            even_w[l] = dict(shared, w_in=w_main, w_alpha=w_al, b_alpha=row(b_alpha[i]),
                             g_gla=row(g_gla_out[i]), w_out=w_out_even[i].astype(BF16))
        else:
            odd_w[l] = dict(shared, w_in=w_in_odd[i].astype(BF16), g_sgu=row(g_sgu_v[i]),
                            b_sgu=row(b_sgu_v[i]), w_sp=w_sp[i], b_sp_t=b_sp[i].T,
                            w_out=w_out_odd[i].astype(BF16))

    mem_k_p, mem_v_p, mem_k_out, mem_v_out = _memkv(mem_prompt, g_mem, w_ck.astype(BF16), w_cv.astype(BF16))
    gfin = row(g_final)

    y_p, sbk_p, sbv_p, gla_p, _ = _run_group(
        x_prompt, None, None, None, mem_k_p, mem_v_p, even_w, odd_w, gfin, False)
    y_s, sbk_s, sbv_s, gla_s, sgu_s = _run_group(
        x_sample, cache_sb_k, cache_sb_v, state_gla, cache_mem_k, cache_mem_v, even_w, odd_w, gfin, True)

    return (y_p, y_s, sbk_p, sbv_p, gla_p, mem_k_out, mem_v_out, sbk_s, sbv_s, gla_s, sgu_s)
```

```python
import functools

import jax
import jax.numpy as jnp
from jax import lax
from jax.experimental import pallas as pl
from jax.experimental.pallas import tpu as pltpu

F32 = jnp.float32
BF16 = jnp.bfloat16

EPS = 1e-6
GLA_HEADS = 4
GLA_DK = 64
GLA_DV = 128
GLA_RANK = 16
GLA_TAU = 16.0
GLA_CHUNK = 64
GLA_KW = GLA_HEADS * GLA_DK
GLA_VW = GLA_HEADS * GLA_DV
SB_HEADS = 8
SB_DIM = 64
SB_W = SB_HEADS * SB_DIM
SB_KEYS = 128
SGU_GROUPS = 4
SGU_CHUNK = 128
X_HEADS = 4
LANES = 128
RA_PAD = LANES
VMEM_LIMIT = 56 * 1024 * 1024
LOG2_E = 1.4426950408889634
CARRY = object()
SB_UNDERFLOW_LOG2 = 160.0
GATE_SLABS = 2
GATE_GAP = 6
PAST_SLAB = 256

C_QA, C_KA, C_VA, C_GA = 0, 256, 512, 1024
C_QB, C_KB, C_VB, C_GB, C_RA, C_END = 1536, 2048, 2560, 3072, 3584, 3712


def _dot(a, b):
    return jnp.dot(a, b, preferred_element_type=F32)


def _dot_nt(a, b):
    return lax.dot_general(a, b, (((1,), (1,)), ((), ())), preferred_element_type=F32)


def _split2(a):
    hi = a.astype(BF16)
    lo = (a - hi.astype(F32)).astype(BF16)
    return hi, lo


def _rms(x, g):
    return x * lax.rsqrt(jnp.mean(x * x, axis=-1, keepdims=True) + EPS) * g


def _sigmoid(x):
    return 1.0 / (1.0 + jnp.exp(-x))


def _silu(x):
    return x * _sigmoid(x)


def _gelu(x):
    return x * (0.5 * (1.0 + jnp.tanh(0.7978845608028654 * (x + 0.044715 * (x * x * x)))))


def _softplus_neg_abs(z):
    return jnp.log1p(jnp.exp(-jnp.abs(z)))


def _iota(shape, dim):
    return lax.broadcasted_iota(jnp.int32, shape, dim)


def _cross_attend(x, g_cross, wq_ref, wo_ref, mkb_ref, mvb_ref):
    d = x.shape[-1]
    hd = d // X_HEADS
    hc = _rms(x, g_cross).astype(BF16)
    q = _dot(hc, wq_ref[...]) * (hd ** -0.5)
    outs = []
    for h in range(X_HEADS):
        sl = slice(h * hd, (h + 1) * hd)
        s = _dot_nt(q[:, sl].astype(BF16), mkb_ref[:, sl])
        e = jnp.exp(s - jnp.max(s, axis=-1, keepdims=True))
        l = jnp.sum(e, axis=-1, keepdims=True)
        outs.append(_dot(e.astype(BF16), mvb_ref[:, sl]) / l)
    o = jnp.concatenate(outs, axis=-1).astype(BF16)
    return x + _dot(o, wo_ref[...])


def _memkv_kernel(m_ref, g_ref, wk_ref, wv_ref, k_ref, v_ref, kh_ref, vh_ref):
    m = _rms(m_ref[0], g_ref[0]).astype(BF16)
    k = _dot(m, wk_ref[0])
    v = _dot(m, wv_ref[0])
    k_ref[0, 0] = k
    v_ref[0, 0] = v
    hd = k.shape[-1] // X_HEADS
    for h in range(X_HEADS):
        kh_ref[0, 0, :, h, :] = k[:, h * hd:(h + 1) * hd]
        vh_ref[0, 0, :, h, :] = v[:, h * hd:(h + 1) * hd]


def _memkv(mem, g_mem, w_ck, w_cv):
    b, n, d = mem.shape
    depth = g_mem.shape[0]
    hd = d // X_HEADS
    flat = jax.ShapeDtypeStruct((depth, b, n, d), F32)
    heads = jax.ShapeDtypeStruct((depth, b, n, X_HEADS, hd), F32)
    wspec = pl.BlockSpec((1, d, d), lambda l, i: (l, 0, 0))
    fspec = pl.BlockSpec((1, 1, n, d), lambda l, i: (l, i, 0, 0))
    hspec = pl.BlockSpec((1, 1, n, X_HEADS, hd), lambda l, i: (l, i, 0, 0, 0))
    return pl.pallas_call(
        _memkv_kernel,
        grid=(depth, b),
        in_specs=[pl.BlockSpec((1, n, d), lambda l, i: (i, 0, 0)),
                  pl.BlockSpec((1, 1, d), lambda l, i: (l, 0, 0)),
                  wspec, wspec],
        out_specs=[fspec, fspec, hspec, hspec],
        out_shape=[flat, flat, heads, heads],
        compiler_params=pltpu.CompilerParams(
            dimension_semantics=("arbitrary", "arbitrary"), vmem_limit_bytes=VMEM_LIMIT),
        name="mem_kv",
    )(mem, g_mem.reshape(depth, 1, d), w_ck, w_cv)


def _even_kernel(*refs, tq, nt_static, past, has_state, n_prev, kv_major):
    it = iter(refs)
    x_ref = next(it)
    pk_ref = next(it) if past else None
    pv_ref = next(it) if past else None
    s0_ref = next(it) if has_state else None
    pko_ref = next(it) if n_prev else None
    pvo_ref = next(it) if n_prev else None
    mk_ref, mv_ref = next(it), next(it)
    gmix_ref, w_ref, wal_ref, bal_ref, ggla_ref, wout_ref, gcr_ref, wq_ref, wo_ref = (
        next(it).at[0] for _ in range(9))
    xo_ref, ko_ref, vo_ref, so_ref = next(it), next(it), next(it), next(it)
    kscr, vscr, mkb, mvb, sst, qs, acc, car = (next(it) for _ in range(8))

    t = pl.program_id(1)
    nt = pl.num_programs(1)
    hist = kscr.shape[1]
    seq = tq * nt_static

    @pl.when(t == 0)
    def _init():
        mkb[...] = mk_ref[0, 0].astype(BF16)
        mvb[...] = mv_ref[0, 0].astype(BF16)
        half0 = (_iota((1, SB_W), 1) % LANES) < SB_DIM
        for c0 in range(0, past, PAST_SLAB):
            c1 = min(c0 + PAST_SLAB, past)
            pk = pk_ref[0, 0, :, c0:c1].T
            pv = pv_ref[0, 0, :, c0:c1].T
            for e in range(2):
                keep_e = half0 if e == 0 else jnp.logical_not(half0)
                kscr[e, c0:c1, :] = jnp.where(keep_e, pk, 0.0).astype(BF16)
                vscr[e, c0:c1, :] = jnp.where(keep_e, pv, 0.0).astype(BF16)
        for e in range(2):
            if hist > past + seq:
                kscr[e, past + seq:hist, :] = jnp.zeros((hist - past - seq, SB_W), BF16)
                vscr[e, past + seq:hist, :] = jnp.zeros((hist - past - seq, SB_W), BF16)
        if has_state:
            for h in range(GLA_HEADS):
                sst[h * GLA_DK:(h + 1) * GLA_DK, :] = s0_ref[0, 0, h]
        else:
            sst[...] = jnp.zeros(sst.shape, F32)

    x = x_ref[0]
    h = _rms(x, gmix_ref[...]).astype(BF16)

    def proj(c0, c1):
        return _dot(h, w_ref[:, c0:c1])

    qb = proj(C_QB, C_KB) * (SB_DIM ** -0.5 * LOG2_E)
    kb = proj(C_KB, C_VB)
    vb = proj(C_VB, C_GB)
    gla_out = []

    def gla_steps():
        ra = proj(C_RA, C_END).astype(BF16)
        yield
        qa = proj(C_QA, C_KA) * (GLA_DK ** -0.5)
        yield
        ka = proj(C_KA, C_VA)
        yield
        va = proj(C_VA, C_GA)
        yield
        za = _dot(ra, wal_ref[...]) + bal_ref[...]
        log_a = (jnp.minimum(za, 0.0) - _softplus_neg_abs(za)) * (1.0 / GLA_TAU)

        cl = GLA_CHUNK
        nc = tq // cl
        row_t = _iota((tq, tq), 0)
        col_t = _iota((tq, tq), 1)
        causal = ((row_t // cl) == (col_t // cl)) & (col_t <= row_t)
        ltri = jnp.where(causal, 1.0, 0.0).astype(BF16)
        g_hi, g_lo = _split2(log_a)
        b = _dot(jnp.concatenate([ltri, ltri], axis=1),
                 jnp.concatenate([g_hi, g_lo], axis=0))
        yield
        bt = b.T
        lane_t = _iota((1, tq), 1)
        b_last = [bt[:, (ci + 1) * cl - 1:(ci + 1) * cl] for ci in range(nc)]
        bl = jnp.broadcast_to(b_last[0], bt.shape)
        for ci in range(1, nc):
            bl = jnp.where(lane_t >= ci * cl, b_last[ci], bl)
        qe = qa * jnp.exp(b)
        ke = (ka * jnp.exp(-b)).astype(BF16)
        kdt = (ka.T * jnp.exp(bl - bt)).astype(BF16)
        vab = va.astype(BF16)
        yield
        lane_kw = _iota((1, GLA_KW), 1)
        zero_b = jnp.zeros((), BF16)
        qeh, o_intra, upd = [], [], []
        for hh in range(GLA_HEADS):
            ks = slice(hh * GLA_DK, (hh + 1) * GLA_DK)
            vs = slice(hh * GLA_DV, (hh + 1) * GLA_DV)
            q_h = jnp.where((lane_kw >= hh * GLA_DK) & (lane_kw < (hh + 1) * GLA_DK), qe, 0.0).astype(BF16)
            qeh.append(q_h)
            att = jnp.where(causal, _dot_nt(q_h, ke), 0.0).astype(BF16)
            o_intra.append(_dot(att, vab[:, vs]))
            yield
            kd_h = kdt[ks]
            kd_c = [jnp.where((lane_t >= ci * cl) & (lane_t < (ci + 1) * cl), kd_h, zero_b) for ci in range(nc)]
            upd.append(_dot(jnp.concatenate(kd_c, axis=0) if nc > 1 else kd_c[0], vab[:, vs]))
            yield
        s_all = sst[...]
        o_inter = []
        for ci in range(nc):
            rs = slice(ci * cl, (ci + 1) * cl)
            q_c = jnp.concatenate([qeh[hh][rs] for hh in range(GLA_HEADS)], axis=0)
            o_inter.append(_dot(q_c, s_all.astype(BF16)))
            u_c = jnp.concatenate([upd[hh][ci * GLA_DK:(ci + 1) * GLA_DK] for hh in range(GLA_HEADS)], axis=0)
            s_all = jnp.exp(b_last[ci]) * s_all + u_c
            yield
        sst[...] = s_all
        o_heads = []
        for hh in range(GLA_HEADS):
            inter = [o_inter[ci][hh * cl:(hh + 1) * cl] for ci in range(nc)]
            o_heads.append(o_intra[hh] + (jnp.concatenate(inter, axis=0) if nc > 1 else inter[0]))
        oa = jnp.concatenate(o_heads, axis=-1)
        ggla = ggla_ref[...]
        oa_n = []
        for hh in range(GLA_HEADS):
            vs = slice(hh * GLA_DV, (hh + 1) * GLA_DV)
            oa_n.append(_rms(oa[:, vs], ggla[:, vs]))
        gla_out.append(jnp.concatenate(oa_n, axis=-1))


    if kv_major:
        for j in range(n_prev):
            ko_ref[j, 0] = pko_ref[j, 0]
            vo_ref[j, 0] = pvo_ref[j, 0]
        ko_ref[n_prev, 0] = kb.T
        vo_ref[n_prev, 0] = vb.T
    else:
        ko_ref[0] = kb
        vo_ref[0] = vb
    half = (_iota((1, SB_W), 1) % LANES) < SB_DIM
    row0 = pl.multiple_of(past + t * tq, min(tq, SB_KEYS))
    for e in range(2):
        keep_e = half if e == 0 else jnp.logical_not(half)
        kscr[e, pl.ds(row0, tq), :] = jnp.where(keep_e, kb, 0.0).astype(BF16)
        vscr[e, pl.ds(row0, tq), :] = jnp.where(keep_e, vb, 0.0).astype(BF16)
    qs[...] = qb.astype(BF16)
    acc[...] = jnp.zeros(acc.shape, F32)
    car[...] = jnp.zeros(car.shape, F32)

    ue_r = _iota((2 * SB_KEYS, 2 * LANES), 0) % SB_KEYS
    ue_c = _iota((2 * SB_KEYS, 2 * LANES), 1)
    ue2 = jnp.where((ue_c < LANES) | (ue_r > ue_c - LANES), 1.0, 0.0).astype(BF16)
    n_pairs = SB_HEADS // 2

    def sb_block_steps(r0, rs, masked, gate=None):
        nr = tq - rs
        keys = pl.ds(r0, SB_KEYS)
        if masked:
            vis1 = _iota((nr, SB_KEYS), 1) < _iota((nr, SB_KEYS), 0)
            vis2 = (_iota((nr, 2 * SB_KEYS), 1) % SB_KEYS) < _iota((nr, 2 * SB_KEYS), 0)
        log_b, lhs = [], []
        for p in range(n_pairs):
            ls_ = slice(p * LANES, (p + 1) * LANES)
            k2 = jnp.concatenate([kscr[0, keys, ls_], kscr[1, keys, ls_]], axis=0)
            z = _dot_nt(qs[rs:tq, ls_], k2)
            sp = jnp.maximum(z, 0.0) + jnp.log2(1.0 + jnp.exp2(-jnp.abs(z)))
            log_b.append(z - sp)
            if masked:
                sp = jnp.where(vis2, sp, 0.0)
            hi, lo = _split2(sp)
            for e in range(2):
                es = slice(e * SB_KEYS, (e + 1) * SB_KEYS)
                lhs.append(jnp.concatenate([hi[:, es], lo[:, es]], axis=1))
            yield
        sc = _dot(jnp.concatenate(lhs, axis=0), ue2)
        yield CARRY
        for p in range(n_pairs):
            ls_ = slice(p * LANES, (p + 1) * LANES)
            ws = []
            for e in range(2):
                hh = 2 * p + e
                sc_h = sc[hh * nr:(hh + 1) * nr]
                carry = car[hh, rs:tq, :]
                w = jnp.exp2(log_b[p][:, e * SB_KEYS:(e + 1) * SB_KEYS] - sc_h[:, LANES:] - carry)
                if masked:
                    w = jnp.where(vis1, w, 0.0)
                ws.append(w.astype(BF16))
                car[hh, rs:tq, :] = carry + sc_h[:, :LANES]
            v2 = jnp.concatenate([vscr[0, keys, ls_], vscr[1, keys, ls_]], axis=0)
            if gate is not None:
                v2 = v2 * gate
            acc[p, rs:tq, :] += _dot(jnp.concatenate(ws, axis=1), v2)
            yield

    def pipelined(blocks):
        def head(g):
            for tok in g:
                if tok is CARRY:
                    return
                yield
        blocks = list(blocks)
        yield from head(blocks[0])
        for k, g in enumerate(blocks):
            if k + 1 < len(blocks):
                yield from head(blocks[k + 1])
            yield from g

    def interleave(streams):
        streams = list(streams)
        while streams:
            for g in list(streams):
                try:
                    next(g)
                except StopIteration:
                    streams.remove(g)

    nblk = (past + t * tq) // SB_KEYS
    step = 2 if (past % (2 * SB_KEYS) == 0 and tq % (2 * SB_KEYS) == 0) else 1

    def below_block(j, gate=None):
        return sb_block_steps(pl.multiple_of(j * SB_KEYS, SB_KEYS), 0, False, gate)

    gate = None if past >= step * SB_KEYS else (nblk >= step).astype(BF16)
    diag = [sb_block_steps(pl.multiple_of(row0 + dblk * SB_KEYS, SB_KEYS), dblk * SB_KEYS, True)
            for dblk in reversed(range(pl.cdiv(tq, SB_KEYS)))]
    first = [below_block(jnp.maximum(nblk - 1 - u, 0), gate) for u in range(step)]
    gates = []

    def gate_steps():
        for c0 in (C_GA, C_GB):
            for k in range(GATE_SLABS):
                for _ in range(GATE_GAP):
                    yield
                w = (C_QB - C_GA) // GATE_SLABS
                gates.append(proj(c0 + k * w, c0 + (k + 1) * w))
                yield

    interleave([pipelined(diag + first), gla_steps(), gate_steps()])
    ga = jnp.concatenate(gates[:GATE_SLABS], axis=-1)
    gb = jnp.concatenate(gates[GATE_SLABS:], axis=-1)
    oa = gla_out[0] * _silu(ga)

    @pl.when(t == nt - 1)
    def _emit_state():
        for hh in range(GLA_HEADS):
            so_ref[0, hh] = sst[hh * GLA_DK:(hh + 1) * GLA_DK, :]

    def still_live():
        lowest = car[0]
        for hh in range(1, SB_HEADS):
            lowest = jnp.minimum(lowest, car[hh])
        return (jnp.min(lowest) < SB_UNDERFLOW_LOG2).astype(jnp.int32)

    def more(c):
        return jnp.logical_and(c[0] < nblk, c[1] > 0)

    def below(c):
        i = c[0]
        interleave([pipelined([below_block(nblk - 1 - i - u) for u in range(step)])])
        return i + step, still_live()

    lax.while_loop(more, below, (jnp.int32(step), still_live()))

    ob = jnp.concatenate([acc[p] for p in range(n_pairs)], axis=-1)
    ob = ob * _silu(gb)

    mix = jnp.concatenate([oa, ob], axis=-1).astype(BF16)
    x = x + _dot(mix, wout_ref[...])
    xo_ref[0] = _cross_attend(x, gcr_ref[...], wq_ref, wo_ref, mkb, mvb)


def _const_spec(stacked, index):
    tail = stacked.shape[1:]
    return pl.BlockSpec((1,) + tail, lambda *_: (index,) + (0,) * len(tail), pipeline_mode=pl.Buffered(1))


def _even_layer(x, past_k, past_v, s0, prev_kv, mem_k, mem_v, layer, wts, tq):
    bsz, t, d = x.shape
    past = 0 if past_k is None else past_k.shape[-1]
    has_state = s0 is not None
    nmem = mem_k.shape[2]
    li = layer // 2
    kv_major = tq % LANES == 0
    n_prev = 0 if prev_kv is None else prev_kv[0].shape[0]
    assert kv_major or prev_kv is None
    assert t % tq == 0 and tq % GLA_CHUNK == 0 and past % SB_KEYS == 0
    assert (tq % SB_KEYS == 0 or t == tq) and SB_KEYS == LANES

    tile = lambda b, i: (b, i, 0)
    args, specs = [x], [pl.BlockSpec((1, tq, d), tile)]
    if past:
        args += [past_k, past_v]
        specs += [pl.BlockSpec((1, 1, SB_W, past), lambda b, i: (li, b, 0, 0))] * 2
    if has_state:
        args.append(s0)
        specs.append(pl.BlockSpec((1, 1, GLA_HEADS, GLA_DK, GLA_DV), lambda b, i: (li, b, 0, 0, 0)))
    if n_prev:
        args += list(prev_kv)
        specs += [pl.BlockSpec((n_prev, 1, SB_W, tq), lambda b, i: (0, b, 0, i))] * 2
    args += [mem_k, mem_v]
    specs += [pl.BlockSpec((1, 1, nmem, d), lambda b, i: (layer, b, 0, 0))] * 2
    for name in ("g_mix", "w_in", "w_alpha", "b_alpha", "g_gla", "w_out", "g_cross", "w_cq", "w_co"):
        stacked, index = wts[name]
        args.append(stacked)
        specs.append(_const_spec(stacked, index))

    if kv_major:
        kv_shape = jax.ShapeDtypeStruct((n_prev + 1, bsz, SB_W, t), F32)
        kv_spec = pl.BlockSpec((n_prev + 1, 1, SB_W, tq), lambda b, i: (0, b, 0, i))
    else:
        kv_shape = jax.ShapeDtypeStruct((bsz, t, SB_W), F32)
        kv_spec = pl.BlockSpec((1, tq, SB_W), tile)
    out_shape = [jax.ShapeDtypeStruct((bsz, t, d), F32), kv_shape, kv_shape,
                 jax.ShapeDtypeStruct((bsz, GLA_HEADS, GLA_DK, GLA_DV), F32)]
    out_specs = [pl.BlockSpec((1, tq, d), tile), kv_spec, kv_spec,
                 pl.BlockSpec((1, GLA_HEADS, GLA_DK, GLA_DV), lambda b, i: (b, 0, 0, 0))]
    hist = past + pl.cdiv(t, SB_KEYS) * SB_KEYS
    scratch = [pltpu.VMEM((2, hist, SB_W), BF16), pltpu.VMEM((2, hist, SB_W), BF16),
               pltpu.VMEM((nmem, d), BF16), pltpu.VMEM((nmem, d), BF16),
               pltpu.VMEM((GLA_KW, GLA_DV), F32),
               pltpu.VMEM((tq, SB_W), BF16),
               pltpu.VMEM((SB_HEADS // 2, tq, LANES), F32),
               pltpu.VMEM((SB_HEADS, tq, LANES), F32)]
    return pl.pallas_call(
        functools.partial(_even_kernel, tq=tq, nt_static=t // tq, past=past, has_state=has_state,
                          n_prev=n_prev, kv_major=kv_major),
        grid=(bsz, t // tq),
        in_specs=specs, out_specs=out_specs, out_shape=out_shape, scratch_shapes=scratch,
        compiler_params=pltpu.CompilerParams(
            dimension_semantics=("arbitrary", "arbitrary"), vmem_limit_bytes=VMEM_LIMIT),
        name="even_layer",
    )(*args)


def _odd_kernel(*refs, tq, cl, emit_vn, final):
    it = iter(refs)
    x_ref, mk_ref, mv_ref = next(it), next(it), next(it)
    gmix_ref, w_ref, gv_ref, bv_ref, wsp_ref, bsp_ref, wout_ref, gcr_ref, wq_ref, wo_ref = (
        next(it).at[0] for _ in range(10))
    gfin_ref = next(it).at[0] if final else None
    xo_ref = next(it)
    vn_ref = next(it) if emit_vn else None
    mkb, mvb = next(it), next(it)

    @pl.when(pl.program_id(1) == 0)
    def _init():
        mkb[...] = mk_ref[0, 0].astype(BF16)
        mvb[...] = mv_ref[0, 0].astype(BF16)

    x = x_ref[0]
    d = x.shape[-1]
    h = _rms(x, gmix_ref[...]).astype(BF16)
    u = _dot(h, w_ref[:, 0:d])
    v = _dot(h, w_ref[:, d:2 * d])
    g = _dot(h, w_ref[:, 2 * d:3 * d])

    gv = _gelu(v)
    xc = gv - jnp.mean(gv, axis=-1, keepdims=True)
    v_n = xc * lax.rsqrt(jnp.mean(xc * xc, axis=-1, keepdims=True) + EPS) * gv_ref[...] + bv_ref[...]
    if emit_vn:
        vn_ref[0] = v_n

    gw = d // SGU_GROUPS
    keep = _iota((cl, cl), 1) <= _iota((cl, cl), 0)
    vnb = v_n.astype(BF16)
    bsp = bsp_ref[0:cl, :]
    wgs = [jnp.where(keep, wsp_ref[gi, 0:cl, 0:cl], 0.0).astype(BF16) for gi in range(SGU_GROUPS)]
    rows = []
    for c in range(tq // cl):
        cols = []
        for gi in range(SGU_GROUPS):
            cols.append(_dot(wgs[gi], vnb[c * cl:(c + 1) * cl, gi * gw:(gi + 1) * gw]) + bsp[:, gi:gi + 1])
        rows.append(jnp.concatenate(cols, axis=-1))
    s = jnp.concatenate(rows, axis=0) if len(rows) > 1 else rows[0]

    y = (_gelu(u) * s * _silu(g)).astype(BF16)
    x = x + _dot(y, wout_ref[...])
    x = _cross_attend(x, gcr_ref[...], wq_ref, wo_ref, mkb, mvb)
    if final:
        x = _rms(x, gfin_ref[...])
    xo_ref[0] = x


def _odd_layer(x, mem_k, mem_v, layer, wts, tq, emit_vn, g_final):
    bsz, t, d = x.shape
    nmem = mem_k.shape[2]
    cl = min(SGU_CHUNK, t)
    final = g_final is not None
    assert t % tq == 0 and tq % cl == 0

    tile = lambda b, i: (b, i, 0)
    args = [x, mem_k, mem_v]
    specs = [pl.BlockSpec((1, tq, d), tile)]
    specs += [pl.BlockSpec((1, 1, nmem, d), lambda b, i: (layer, b, 0, 0))] * 2
    consts = [wts[name] for name in
              ("g_mix", "w_in", "g_sgu", "b_sgu", "w_sp", "b_sp_t", "w_out", "g_cross", "w_cq", "w_co")]
    if final:
        consts.append((g_final, 0))
    for stacked, index in consts:
        args.append(stacked)
        specs.append(_const_spec(stacked, index))

    out_shape = [jax.ShapeDtypeStruct((bsz, t, d), F32)]
    out_specs = [pl.BlockSpec((1, tq, d), tile)]
    if emit_vn:
        out_shape.append(jax.ShapeDtypeStruct((bsz, t, d), F32))
        out_specs.append(pl.BlockSpec((1, tq, d), tile))
    return pl.pallas_call(
        functools.partial(_odd_kernel, tq=tq, cl=cl, emit_vn=emit_vn, final=final),
        grid=(bsz, t // tq),
        in_specs=specs, out_specs=out_specs, out_shape=out_shape,
        scratch_shapes=[pltpu.VMEM((nmem, d), BF16), pltpu.VMEM((nmem, d), BF16)],
        compiler_params=pltpu.CompilerParams(
            dimension_semantics=("arbitrary", "arbitrary"), vmem_limit_bytes=VMEM_LIMIT),
        name="odd_layer",
    )(*args)


EVEN_TILE = 256
ODD_TILE = 512


def _tile_rows(t, layer):
    return min(EVEN_TILE if layer % 2 == 0 else ODD_TILE, t)


def _run_group(x, past_k, past_v, s0, mem_k, mem_v, even_w, odd_w, g_final, emit_vn):
    depth = mem_k.shape[0]
    bsz, t, d = x.shape
    tq = _tile_rows(t, 0)
    mem_k = mem_k.reshape(depth, bsz, -1, d)
    mem_v = mem_v.reshape(depth, bsz, -1, d)
    if past_k is not None:
        n_even, _, past = past_k.shape[:3]
        past_k = past_k.transpose(0, 1, 3, 4, 2).reshape(n_even, bsz, SB_W, past)
        past_v = past_v.transpose(0, 1, 3, 4, 2).reshape(n_even, bsz, SB_W, past)
    kv_major = tq % LANES == 0
    sb_k, sb_v, gla_s, sgu_v = [], [], [], []
    prev_kv = None
    for l in range(depth):
        if l % 2 == 0:
            x, kb, vb, s_new = _even_layer(x, past_k, past_v, s0, prev_kv, mem_k, mem_v, l, even_w[l], tq)
            if kv_major:
                prev_kv = (kb, vb)
            else:
                sb_k.append(kb.reshape(bsz, t, SB_HEADS, SB_DIM))
                sb_v.append(vb.reshape(bsz, t, SB_HEADS, SB_DIM))
            gla_s.append(s_new)
        else:
            outs = _odd_layer(x, mem_k, mem_v, l, odd_w[l], _tile_rows(t, l), emit_vn,
                              g_final if l == depth - 1 else None)
            x = outs[0]
            if emit_vn:
                sgu_v.append(outs[1])
    if kv_major:
        sb_k, sb_v = (a.reshape(a.shape[0], bsz, SB_HEADS, SB_DIM, t).transpose(0, 1, 4, 2, 3) for a in prev_kv)
    else:
        sb_k, sb_v = jnp.stack(sb_k), jnp.stack(sb_v)
    return x, sb_k, sb_v, jnp.stack(gla_s), (jnp.stack(sgu_v) if emit_vn else None)


def kernel(x_prompt, x_sample, cache_sb_k, cache_sb_v, state_gla, cache_mem_k, cache_mem_v, mem_prompt, g_mix, w_in_even, w_alpha, b_alpha, g_gla_out, w_out_even, w_in_odd, g_sgu_v, b_sgu_v, w_sp, b_sp, w_out_odd, g_cross, g_mem, w_cq, w_ck, w_cv, w_co, g_final):
    depth, d = g_mix.shape
    vec = lambda a: a[:, None, :]

    o_ra = C_QB
    w_main = jnp.concatenate(
        [w_in_even[:, :, :o_ra], w_in_even[:, :, o_ra + GLA_RANK:], w_in_even[:, :, o_ra:o_ra + GLA_RANK],
         jnp.zeros(w_in_even.shape[:2] + (RA_PAD - GLA_RANK,), w_in_even.dtype)], axis=2).astype(BF16)
    w_al = jnp.concatenate(
        [w_alpha, jnp.zeros((w_alpha.shape[0], RA_PAD - GLA_RANK, GLA_KW), w_alpha.dtype)], axis=1).astype(BF16)
    stacks = dict(
        g_mix=vec(g_mix), g_cross=vec(g_cross), w_cq=w_cq.astype(BF16), w_co=w_co.astype(BF16),
        w_in_e=w_main, w_alpha=w_al, b_alpha=vec(b_alpha), g_gla=vec(g_gla_out), w_out_e=w_out_even.astype(BF16),
        w_in_o=w_in_odd.astype(BF16), g_sgu=vec(g_sgu_v), b_sgu=vec(b_sgu_v), w_sp=w_sp,
        b_sp_t=b_sp.transpose(0, 2, 1), w_out_o=w_out_odd.astype(BF16))
    even_w, odd_w = {}, {}
    for l in range(depth):
        i = l // 2
        shared = {k: (stacks[k], l) for k in ("g_mix", "g_cross", "w_cq", "w_co")}
        if l % 2 == 0:
            even_w[l] = dict(shared, w_in=(stacks["w_in_e"], i), w_alpha=(stacks["w_alpha"], i),
                             b_alpha=(stacks["b_alpha"], i), g_gla=(stacks["g_gla"], i),
                             w_out=(stacks["w_out_e"], i))
        else:
            odd_w[l] = dict(shared, w_in=(stacks["w_in_o"], i), g_sgu=(stacks["g_sgu"], i),
                            b_sgu=(stacks["b_sgu"], i), w_sp=(stacks["w_sp"], i),
                            b_sp_t=(stacks["b_sp_t"], i), w_out=(stacks["w_out_o"], i))

    mem_k_p, mem_v_p, mem_k_out, mem_v_out = _memkv(mem_prompt, g_mem, w_ck.astype(BF16), w_cv.astype(BF16))
    gfin = g_final.reshape(1, 1, -1)

    y_p, sbk_p, sbv_p, gla_p, _ = _run_group(
        x_prompt, None, None, None, mem_k_p, mem_v_p, even_w, odd_w, gfin, False)
    y_s, sbk_s, sbv_s, gla_s, sgu_s = _run_group(
        x_sample, cache_sb_k, cache_sb_v, state_gla, cache_mem_k, cache_mem_v, even_w, odd_w, gfin, True)

    return (y_p, y_s, sbk_p, sbv_p, gla_p, mem_k_out, mem_v_out, sbk_s, sbv_s, gla_s, sgu_s)
```

```python
import functools

import jax
import jax.numpy as jnp
from jax import lax
from jax.experimental import pallas as pl
from jax.experimental.pallas import tpu as pltpu

F32 = jnp.float32
BF16 = jnp.bfloat16

EPS = 1e-6
GLA_HEADS = 4
GLA_DK = 64
GLA_DV = 128
GLA_RANK = 16
GLA_TAU = 16.0
GLA_CHUNK = 64
GLA_KW = GLA_HEADS * GLA_DK
GLA_VW = GLA_HEADS * GLA_DV
SB_HEADS = 8
SB_DIM = 64
SB_W = SB_HEADS * SB_DIM
SB_KEYS = 128
SGU_GROUPS = 4
SGU_CHUNK = 128
X_HEADS = 4
LANES = 128
RA_PAD = LANES
VMEM_LIMIT = 56 * 1024 * 1024
LOG2_E = 1.4426950408889634
SB_MASKED_SCORE = -1e30
CARRY = object()
SB_UNDERFLOW_LOG2 = 160.0
GATE_SLABS = 2
GATE_GAP = 6
PAST_SLAB = 256

C_QA, C_KA, C_VA, C_GA = 0, 256, 512, 1024
C_QB, C_KB, C_VB, C_GB, C_RA, C_END = 1536, 2048, 2560, 3072, 3584, 3712


def _dot(a, b):
    return jnp.dot(a, b, preferred_element_type=F32)


def _dot_nt(a, b):
    return lax.dot_general(a, b, (((1,), (1,)), ((), ())), preferred_element_type=F32)


def _split2(a):
    hi = a.astype(BF16)
    lo = (a - hi.astype(F32)).astype(BF16)
    return hi, lo


def _rms(x, g):
    return x * lax.rsqrt(jnp.mean(x * x, axis=-1, keepdims=True) + EPS) * g


def _sigmoid(x):
    return 1.0 / (1.0 + jnp.exp(-x))


def _silu(x):
    return x * _sigmoid(x)


def _gelu(x):
    return x * (0.5 * (1.0 + jnp.tanh(0.7978845608028654 * (x + 0.044715 * (x * x * x)))))


def _softplus_neg_abs(z):
    return jnp.log1p(jnp.exp(-jnp.abs(z)))


def _iota(shape, dim):
    return lax.broadcasted_iota(jnp.int32, shape, dim)


def _cross_attend(x, g_cross, wq_ref, wo_ref, mkb_ref, mvb_ref):
    d = x.shape[-1]
    hd = d // X_HEADS
    hc = _rms(x, g_cross).astype(BF16)
    q = _dot(hc, wq_ref[...]) * (hd ** -0.5)
    outs = []
    for h in range(X_HEADS):
        sl = slice(h * hd, (h + 1) * hd)
        s = _dot_nt(q[:, sl].astype(BF16), mkb_ref[:, sl])
        e = jnp.exp(s - jnp.max(s, axis=-1, keepdims=True))
        l = jnp.sum(e, axis=-1, keepdims=True)
        outs.append(_dot(e.astype(BF16), mvb_ref[:, sl]) / l)
    o = jnp.concatenate(outs, axis=-1).astype(BF16)
    return x + _dot(o, wo_ref[...])


def _memkv_kernel(m_ref, g_ref, wk_ref, wv_ref, k_ref, v_ref, kh_ref, vh_ref):
    m = _rms(m_ref[0], g_ref[0]).astype(BF16)
    k = _dot(m, wk_ref[0])
    v = _dot(m, wv_ref[0])
    k_ref[0, 0] = k
    v_ref[0, 0] = v
    hd = k.shape[-1] // X_HEADS
    for h in range(X_HEADS):
        kh_ref[0, 0, :, h, :] = k[:, h * hd:(h + 1) * hd]
        vh_ref[0, 0, :, h, :] = v[:, h * hd:(h + 1) * hd]


def _memkv(mem, g_mem, w_ck, w_cv):
    b, n, d = mem.shape
    depth = g_mem.shape[0]
    hd = d // X_HEADS
    flat = jax.ShapeDtypeStruct((depth, b, n, d), F32)
    heads = jax.ShapeDtypeStruct((depth, b, n, X_HEADS, hd), F32)
    wspec = pl.BlockSpec((1, d, d), lambda l, i: (l, 0, 0))
    fspec = pl.BlockSpec((1, 1, n, d), lambda l, i: (l, i, 0, 0))
    hspec = pl.BlockSpec((1, 1, n, X_HEADS, hd), lambda l, i: (l, i, 0, 0, 0))
    return pl.pallas_call(
        _memkv_kernel,
        grid=(depth, b),
        in_specs=[pl.BlockSpec((1, n, d), lambda l, i: (i, 0, 0)),
                  pl.BlockSpec((1, 1, d), lambda l, i: (l, 0, 0)),
                  wspec, wspec],
        out_specs=[fspec, fspec, hspec, hspec],
        out_shape=[flat, flat, heads, heads],
        compiler_params=pltpu.CompilerParams(
            dimension_semantics=("arbitrary", "arbitrary"), vmem_limit_bytes=VMEM_LIMIT),
        name="mem_kv",
    )(mem, g_mem.reshape(depth, 1, d), w_ck, w_cv)


def _even_kernel(*refs, tq, nt_static, past, has_state, n_prev, kv_major):
    it = iter(refs)
    x_ref = next(it)
    pk_ref = next(it) if past else None
    pv_ref = next(it) if past else None
    s0_ref = next(it) if has_state else None
    pko_ref = next(it) if n_prev else None
    pvo_ref = next(it) if n_prev else None
    mk_ref, mv_ref = next(it), next(it)
    gmix_ref, w_ref, wal_ref, bal_ref, ggla_ref, wout_ref, gcr_ref, wq_ref, wo_ref = (
        next(it).at[0] for _ in range(9))
    xo_ref, ko_ref, vo_ref, so_ref = next(it), next(it), next(it), next(it)
    kscr, vscr, mkb, mvb, sst, qs, acc, car = (next(it) for _ in range(8))

    t = pl.program_id(1)
    nt = pl.num_programs(1)
    hist = kscr.shape[1]
    seq = tq * nt_static

    @pl.when(t == 0)
    def _init():
        mkb[...] = mk_ref[0, 0].astype(BF16)
        mvb[...] = mv_ref[0, 0].astype(BF16)
        half0 = (_iota((1, SB_W), 1) % LANES) < SB_DIM
        for c0 in range(0, past, PAST_SLAB):
            c1 = min(c0 + PAST_SLAB, past)
            pk = pk_ref[0, 0, :, c0:c1].T
            pv = pv_ref[0, 0, :, c0:c1].T
            for e in range(2):
                keep_e = half0 if e == 0 else jnp.logical_not(half0)
                kscr[e, c0:c1, :] = jnp.where(keep_e, pk, 0.0).astype(BF16)
                vscr[e, c0:c1, :] = jnp.where(keep_e, pv, 0.0).astype(BF16)
        for e in range(2):
            if hist > past + seq:
                kscr[e, past + seq:hist, :] = jnp.zeros((hist - past - seq, SB_W), BF16)
                vscr[e, past + seq:hist, :] = jnp.zeros((hist - past - seq, SB_W), BF16)
        if has_state:
            for h in range(GLA_HEADS):
                sst[h * GLA_DK:(h + 1) * GLA_DK, :] = s0_ref[0, 0, h]
        else:
            sst[...] = jnp.zeros(sst.shape, F32)

    x = x_ref[0]
    h = _rms(x, gmix_ref[...]).astype(BF16)

    def proj(c0, c1):
        return _dot(h, w_ref[:, c0:c1])

    qb = proj(C_QB, C_KB) * (SB_DIM ** -0.5 * LOG2_E)
    kb = proj(C_KB, C_VB)
    vb = proj(C_VB, C_GB)
    gla_out = []

    def gla_steps():
        ra = proj(C_RA, C_END).astype(BF16)
        yield
        qa = proj(C_QA, C_KA) * (GLA_DK ** -0.5)
        yield
        ka = proj(C_KA, C_VA)
        yield
        va = proj(C_VA, C_GA)
        yield
        za = _dot(ra, wal_ref[...]) + bal_ref[...]
        log_a = (jnp.minimum(za, 0.0) - _softplus_neg_abs(za)) * (1.0 / GLA_TAU)

        cl = GLA_CHUNK
        nc = tq // cl
        row_t = _iota((tq, tq), 0)
        col_t = _iota((tq, tq), 1)
        causal = ((row_t // cl) == (col_t // cl)) & (col_t <= row_t)
        ltri = jnp.where(causal, 1.0, 0.0).astype(BF16)
        g_hi, g_lo = _split2(log_a)
        b = _dot(jnp.concatenate([ltri, ltri], axis=1),
                 jnp.concatenate([g_hi, g_lo], axis=0))
        yield
        bt = b.T
        lane_t = _iota((1, tq), 1)
        b_last = [bt[:, (ci + 1) * cl - 1:(ci + 1) * cl] for ci in range(nc)]
        bl = jnp.broadcast_to(b_last[0], bt.shape)
        for ci in range(1, nc):
            bl = jnp.where(lane_t >= ci * cl, b_last[ci], bl)
        qeb = (qa * jnp.exp(b)).astype(BF16)
        ke = (ka * jnp.exp(-b)).astype(BF16)
        kdt = (ka.T * jnp.exp(bl - bt)).astype(BF16)
        vab = va.astype(BF16)
        yield
        lane_kw = _iota((1, GLA_KW), 1)
        zero_b = jnp.zeros((), BF16)
        qeh, o_intra, upd = [], [], []
        for hh in range(GLA_HEADS):
            ks = slice(hh * GLA_DK, (hh + 1) * GLA_DK)
            vs = slice(hh * GLA_DV, (hh + 1) * GLA_DV)
            q_h = jnp.where((lane_kw >= hh * GLA_DK) & (lane_kw < (hh + 1) * GLA_DK), qeb, zero_b)
            qeh.append(q_h)
            att = jnp.where(causal, _dot_nt(q_h, ke), 0.0).astype(BF16)
            o_intra.append(_dot(att, vab[:, vs]))
            yield
            kd_h = kdt[ks]
            kd_c = [jnp.where((lane_t >= ci * cl) & (lane_t < (ci + 1) * cl), kd_h, zero_b) for ci in range(nc)]
            upd.append(_dot(jnp.concatenate(kd_c, axis=0) if nc > 1 else kd_c[0], vab[:, vs]))
            yield
        s_all = sst[...]
        o_inter = []
        for ci in range(nc):
            rs = slice(ci * cl, (ci + 1) * cl)
            q_c = jnp.concatenate([qeh[hh][rs] for hh in range(GLA_HEADS)], axis=0)
            o_inter.append(_dot(q_c, s_all.astype(BF16)))
            u_c = jnp.concatenate([upd[hh][ci * GLA_DK:(ci + 1) * GLA_DK] for hh in range(GLA_HEADS)], axis=0)
            s_all = jnp.exp(b_last[ci]) * s_all + u_c
            yield
        sst[...] = s_all
        o_heads = []
        for hh in range(GLA_HEADS):
            inter = [o_inter[ci][hh * cl:(hh + 1) * cl] for ci in range(nc)]
            o_heads.append(o_intra[hh] + (jnp.concatenate(inter, axis=0) if nc > 1 else inter[0]))
        oa = jnp.concatenate(o_heads, axis=-1)
        ggla = ggla_ref[...]
        oa_n = []
        for hh in range(GLA_HEADS):
            vs = slice(hh * GLA_DV, (hh + 1) * GLA_DV)
            oa_n.append(_rms(oa[:, vs], ggla[:, vs]))
        gla_out.append(jnp.concatenate(oa_n, axis=-1))


    if kv_major:
        for j in range(n_prev):
            ko_ref[j, 0] = pko_ref[j, 0]
            vo_ref[j, 0] = pvo_ref[j, 0]
        ko_ref[n_prev, 0] = kb.T
        vo_ref[n_prev, 0] = vb.T
    else:
        ko_ref[0] = kb
        vo_ref[0] = vb
    half = (_iota((1, SB_W), 1) % LANES) < SB_DIM
    row0 = pl.multiple_of(past + t * tq, min(tq, SB_KEYS))
    kbb, vbb = kb.astype(BF16), vb.astype(BF16)
    zero_bf = jnp.zeros((), BF16)
    for e in range(2):
        keep_e = half if e == 0 else jnp.logical_not(half)
        kscr[e, pl.ds(row0, tq), :] = jnp.where(keep_e, kbb, zero_bf)
        vscr[e, pl.ds(row0, tq), :] = jnp.where(keep_e, vbb, zero_bf)
    qs[...] = qb.astype(BF16)
    acc[...] = jnp.zeros(acc.shape, F32)
    car[...] = jnp.zeros(car.shape, F32)

    ue_r = _iota((2 * SB_KEYS, 2 * LANES), 0) % SB_KEYS
    ue_c = _iota((2 * SB_KEYS, 2 * LANES), 1)
    ue2 = jnp.where((ue_c < LANES) | (ue_r >= ue_c - LANES), 1.0, 0.0).astype(BF16)
    n_pairs = SB_HEADS // 2

    def sb_block_steps(r0, rs, masked, gate=None):
        nr = tq - rs
        keys = pl.ds(r0, SB_KEYS)
        if masked:
            vis2 = (_iota((nr, 2 * SB_KEYS), 1) % SB_KEYS) < _iota((nr, 2 * SB_KEYS), 0)
        zs, lhs = [], []
        for p in range(n_pairs):
            ls_ = slice(p * LANES, (p + 1) * LANES)
            k2 = jnp.concatenate([kscr[0, keys, ls_], kscr[1, keys, ls_]], axis=0)
            z = _dot_nt(qs[rs:tq, ls_], k2)
            if masked:
                z = jnp.where(vis2, z, SB_MASKED_SCORE)
            sp = jnp.maximum(z, 0.0) + jnp.log2(1.0 + jnp.exp2(-jnp.abs(z)))
            zs.append(z)
            hi, lo = _split2(sp)
            for e in range(2):
                es = slice(e * SB_KEYS, (e + 1) * SB_KEYS)
                lhs.append(jnp.concatenate([hi[:, es], lo[:, es]], axis=1))
            yield
        sc = _dot(jnp.concatenate(lhs, axis=0), ue2)
        yield CARRY
        for p in range(n_pairs):
            ls_ = slice(p * LANES, (p + 1) * LANES)
            ws = []
            for e in range(2):
                hh = 2 * p + e
                sc_h = sc[hh * nr:(hh + 1) * nr]
                carry = car[hh, rs:tq, :]
                w = jnp.exp2(zs[p][:, e * SB_KEYS:(e + 1) * SB_KEYS] - sc_h[:, LANES:] - carry)
                ws.append(w.astype(BF16))
                car[hh, rs:tq, :] = carry + sc_h[:, :LANES]
            v2 = jnp.concatenate([vscr[0, keys, ls_], vscr[1, keys, ls_]], axis=0)
            if gate is not None:
                v2 = v2 * gate
            acc[p, rs:tq, :] += _dot(jnp.concatenate(ws, axis=1), v2)
            yield

    def pipelined(blocks):
        def head(g):
            for tok in g:
                if tok is CARRY:
                    return
                yield
        blocks = list(blocks)
        yield from head(blocks[0])
        for k, g in enumerate(blocks):
            if k + 1 < len(blocks):
                yield from head(blocks[k + 1])
            yield from g

    def interleave(streams):
        streams = list(streams)
        while streams:
            for g in list(streams):
                try:
                    next(g)
                except StopIteration:
                    streams.remove(g)

    nblk = (past + t * tq) // SB_KEYS
    step = 2 if (past % (2 * SB_KEYS) == 0 and tq % (2 * SB_KEYS) == 0) else 1

    def below_block(j, gate=None):
        return sb_block_steps(pl.multiple_of(j * SB_KEYS, SB_KEYS), 0, False, gate)

    gate = None if past >= step * SB_KEYS else (nblk >= step).astype(BF16)
    diag = [sb_block_steps(pl.multiple_of(row0 + dblk * SB_KEYS, SB_KEYS), dblk * SB_KEYS, True)
            for dblk in reversed(range(pl.cdiv(tq, SB_KEYS)))]
    first = [below_block(jnp.maximum(nblk - 1 - u, 0), gate) for u in range(step)]
    gates = []

    def gate_steps():
        for c0 in (C_GA, C_GB):
            for k in range(GATE_SLABS):
                for _ in range(GATE_GAP):
                    yield
                w = (C_QB - C_GA) // GATE_SLABS
                gates.append(proj(c0 + k * w, c0 + (k + 1) * w))
                yield

    interleave([pipelined(diag + first), gla_steps(), gate_steps()])
    ga = jnp.concatenate(gates[:GATE_SLABS], axis=-1)
    gb = jnp.concatenate(gates[GATE_SLABS:], axis=-1)
    oa = gla_out[0] * _silu(ga)

    @pl.when(t == nt - 1)
    def _emit_state():
        for hh in range(GLA_HEADS):
            so_ref[0, hh] = sst[hh * GLA_DK:(hh + 1) * GLA_DK, :]

    def still_live():
        lowest = car[0]
        for hh in range(1, SB_HEADS):
            lowest = jnp.minimum(lowest, car[hh])
        return (jnp.min(lowest) < SB_UNDERFLOW_LOG2).astype(jnp.int32)

    def more(c):
        return jnp.logical_and(c[0] < nblk, c[1] > 0)

    def below(c):
        i = c[0]
        interleave([pipelined([below_block(nblk - 1 - i - u) for u in range(step)])])
        return i + step, still_live()

    lax.while_loop(more, below, (jnp.int32(step), still_live()))

    ob = jnp.concatenate([acc[p] for p in range(n_pairs)], axis=-1)
    ob = ob * _silu(gb)

    mix = jnp.concatenate([oa, ob], axis=-1).astype(BF16)
    x = x + _dot(mix, wout_ref[...])
    xo_ref[0] = _cross_attend(x, gcr_ref[...], wq_ref, wo_ref, mkb, mvb)


def _const_spec(stacked, index):
    tail = stacked.shape[1:]
    return pl.BlockSpec((1,) + tail, lambda *_: (index,) + (0,) * len(tail), pipeline_mode=pl.Buffered(1))


def _even_layer(x, past_k, past_v, s0, prev_kv, mem_k, mem_v, layer, wts, tq):
    bsz, t, d = x.shape
    past = 0 if past_k is None else past_k.shape[-1]
    has_state = s0 is not None
    nmem = mem_k.shape[2]
    li = layer // 2
    kv_major = tq % LANES == 0
    n_prev = 0 if prev_kv is None else prev_kv[0].shape[0]
    assert kv_major or prev_kv is None
    assert t % tq == 0 and tq % GLA_CHUNK == 0 and past % SB_KEYS == 0
    assert (tq % SB_KEYS == 0 or t == tq) and SB_KEYS == LANES

    tile = lambda b, i: (b, i, 0)
    args, specs = [x], [pl.BlockSpec((1, tq, d), tile)]
    if past:
        args += [past_k, past_v]
        specs += [pl.BlockSpec((1, 1, SB_W, past), lambda b, i: (li, b, 0, 0))] * 2
    if has_state:
        args.append(s0)
        specs.append(pl.BlockSpec((1, 1, GLA_HEADS, GLA_DK, GLA_DV), lambda b, i: (li, b, 0, 0, 0)))
    if n_prev:
        args += list(prev_kv)
        specs += [pl.BlockSpec((n_prev, 1, SB_W, tq), lambda b, i: (0, b, 0, i))] * 2
    args += [mem_k, mem_v]
    specs += [pl.BlockSpec((1, 1, nmem, d), lambda b, i: (layer, b, 0, 0))] * 2
    for name in ("g_mix", "w_in", "w_alpha", "b_alpha", "g_gla", "w_out", "g_cross", "w_cq", "w_co"):
        stacked, index = wts[name]
        args.append(stacked)
        specs.append(_const_spec(stacked, index))

    if kv_major:
        kv_shape = jax.ShapeDtypeStruct((n_prev + 1, bsz, SB_W, t), F32)
        kv_spec = pl.BlockSpec((n_prev + 1, 1, SB_W, tq), lambda b, i: (0, b, 0, i))
    else:
        kv_shape = jax.ShapeDtypeStruct((bsz, t, SB_W), F32)
        kv_spec = pl.BlockSpec((1, tq, SB_W), tile)
    out_shape = [jax.ShapeDtypeStruct((bsz, t, d), F32), kv_shape, kv_shape,
                 jax.ShapeDtypeStruct((bsz, GLA_HEADS, GLA_DK, GLA_DV), F32)]
    out_specs = [pl.BlockSpec((1, tq, d), tile), kv_spec, kv_spec,
                 pl.BlockSpec((1, GLA_HEADS, GLA_DK, GLA_DV), lambda b, i: (b, 0, 0, 0))]
    hist = past + pl.cdiv(t, SB_KEYS) * SB_KEYS
    scratch = [pltpu.VMEM((2, hist, SB_W), BF16), pltpu.VMEM((2, hist, SB_W), BF16),
               pltpu.VMEM((nmem, d), BF16), pltpu.VMEM((nmem, d), BF16),
               pltpu.VMEM((GLA_KW, GLA_DV), F32),
               pltpu.VMEM((tq, SB_W), BF16),
               pltpu.VMEM((SB_HEADS // 2, tq, LANES), F32),
               pltpu.VMEM((SB_HEADS, tq, LANES), F32)]
    return pl.pallas_call(
        functools.partial(_even_kernel, tq=tq, nt_static=t // tq, past=past, has_state=has_state,
                          n_prev=n_prev, kv_major=kv_major),
        grid=(bsz, t // tq),
        in_specs=specs, out_specs=out_specs, out_shape=out_shape, scratch_shapes=scratch,
        compiler_params=pltpu.CompilerParams(
            dimension_semantics=("arbitrary", "arbitrary"), vmem_limit_bytes=VMEM_LIMIT),
        name="even_layer",
    )(*args)


def _odd_kernel(*refs, tq, cl, emit_vn, final):
    it = iter(refs)
    x_ref, mk_ref, mv_ref = next(it), next(it), next(it)
    gmix_ref, w_ref, gv_ref, bv_ref, wsp_ref, bsp_ref, wout_ref, gcr_ref, wq_ref, wo_ref = (
        next(it).at[0] for _ in range(10))
    gfin_ref = next(it).at[0] if final else None
    xo_ref = next(it)
    vn_ref = next(it) if emit_vn else None
    mkb, mvb = next(it), next(it)

    @pl.when(pl.program_id(1) == 0)
    def _init():
        mkb[...] = mk_ref[0, 0].astype(BF16)
        mvb[...] = mv_ref[0, 0].astype(BF16)

    x = x_ref[0]
    d = x.shape[-1]
    h = _rms(x, gmix_ref[...]).astype(BF16)
    u = _dot(h, w_ref[:, 0:d])
    v = _dot(h, w_ref[:, d:2 * d])
    g = _dot(h, w_ref[:, 2 * d:3 * d])

    gv = _gelu(v)
    xc = gv - jnp.mean(gv, axis=-1, keepdims=True)
    v_n = xc * lax.rsqrt(jnp.mean(xc * xc, axis=-1, keepdims=True) + EPS) * gv_ref[...] + bv_ref[...]
    if emit_vn:
        vn_ref[0] = v_n

    gw = d // SGU_GROUPS
    keep = _iota((cl, cl), 1) <= _iota((cl, cl), 0)
    vnb = v_n.astype(BF16)
    bsp = bsp_ref[0:cl, :]
    wgs = [jnp.where(keep, wsp_ref[gi, 0:cl, 0:cl], 0.0).astype(BF16) for gi in range(SGU_GROUPS)]
    rows = []
    for c in range(tq // cl):
        cols = []
        for gi in range(SGU_GROUPS):
            cols.append(_dot(wgs[gi], vnb[c * cl:(c + 1) * cl, gi * gw:(gi + 1) * gw]) + bsp[:, gi:gi + 1])
        rows.append(jnp.concatenate(cols, axis=-1))
    s = jnp.concatenate(rows, axis=0) if len(rows) > 1 else rows[0]

    y = (_gelu(u) * s * _silu(g)).astype(BF16)
    x = x + _dot(y, wout_ref[...])
    x = _cross_attend(x, gcr_ref[...], wq_ref, wo_ref, mkb, mvb)
    if final:
        x = _rms(x, gfin_ref[...])
    xo_ref[0] = x


def _odd_layer(x, mem_k, mem_v, layer, wts, tq, emit_vn, g_final):
    bsz, t, d = x.shape
    nmem = mem_k.shape[2]
    cl = min(SGU_CHUNK, t)
    final = g_final is not None
    assert t % tq == 0 and tq % cl == 0

    tile = lambda b, i: (b, i, 0)
    args = [x, mem_k, mem_v]
    specs = [pl.BlockSpec((1, tq, d), tile)]
    specs += [pl.BlockSpec((1, 1, nmem, d), lambda b, i: (layer, b, 0, 0))] * 2
    consts = [wts[name] for name in
              ("g_mix", "w_in", "g_sgu", "b_sgu", "w_sp", "b_sp_t", "w_out", "g_cross", "w_cq", "w_co")]
    if final:
        consts.append((g_final, 0))
    for stacked, index in consts:
        args.append(stacked)
        specs.append(_const_spec(stacked, index))

    out_shape = [jax.ShapeDtypeStruct((bsz, t, d), F32)]
    out_specs = [pl.BlockSpec((1, tq, d), tile)]
    if emit_vn:
        out_shape.append(jax.ShapeDtypeStruct((bsz, t, d), F32))
        out_specs.append(pl.BlockSpec((1, tq, d), tile))
    return pl.pallas_call(
        functools.partial(_odd_kernel, tq=tq, cl=cl, emit_vn=emit_vn, final=final),
        grid=(bsz, t // tq),
        in_specs=specs, out_specs=out_specs, out_shape=out_shape,
        scratch_shapes=[pltpu.VMEM((nmem, d), BF16), pltpu.VMEM((nmem, d), BF16)],
        compiler_params=pltpu.CompilerParams(
            dimension_semantics=("arbitrary", "arbitrary"), vmem_limit_bytes=VMEM_LIMIT),
        name="odd_layer",
    )(*args)


EVEN_TILE = 256
ODD_TILE = 512


def _tile_rows(t, layer):
    return min(EVEN_TILE if layer % 2 == 0 else ODD_TILE, t)


def _run_group(x, past_k, past_v, s0, mem_k, mem_v, even_w, odd_w, g_final, emit_vn):
    depth = mem_k.shape[0]
    bsz, t, d = x.shape
    tq = _tile_rows(t, 0)
    mem_k = mem_k.reshape(depth, bsz, -1, d)
    mem_v = mem_v.reshape(depth, bsz, -1, d)
    if past_k is not None:
        n_even, _, past = past_k.shape[:3]
        past_k = past_k.transpose(0, 1, 3, 4, 2).reshape(n_even, bsz, SB_W, past)
        past_v = past_v.transpose(0, 1, 3, 4, 2).reshape(n_even, bsz, SB_W, past)
    kv_major = tq % LANES == 0
    sb_k, sb_v, gla_s, sgu_v = [], [], [], []
    prev_kv = None
    for l in range(depth):
        if l % 2 == 0:
            x, kb, vb, s_new = _even_layer(x, past_k, past_v, s0, prev_kv, mem_k, mem_v, l, even_w[l], tq)
            if kv_major:
                prev_kv = (kb, vb)
            else:
                sb_k.append(kb.reshape(bsz, t, SB_HEADS, SB_DIM))
                sb_v.append(vb.reshape(bsz, t, SB_HEADS, SB_DIM))
            gla_s.append(s_new)
        else:
            outs = _odd_layer(x, mem_k, mem_v, l, odd_w[l], _tile_rows(t, l), emit_vn,
                              g_final if l == depth - 1 else None)
            x = outs[0]
            if emit_vn:
                sgu_v.append(outs[1])
    if kv_major:
        sb_k, sb_v = (a.reshape(a.shape[0], bsz, SB_HEADS, SB_DIM, t).transpose(0, 1, 4, 2, 3) for a in prev_kv)
    else:
        sb_k, sb_v = jnp.stack(sb_k), jnp.stack(sb_v)
    return x, sb_k, sb_v, jnp.stack(gla_s), (jnp.stack(sgu_v) if emit_vn else None)


def kernel(x_prompt, x_sample, cache_sb_k, cache_sb_v, state_gla, cache_mem_k, cache_mem_v, mem_prompt, g_mix, w_in_even, w_alpha, b_alpha, g_gla_out, w_out_even, w_in_odd, g_sgu_v, b_sgu_v, w_sp, b_sp, w_out_odd, g_cross, g_mem, w_cq, w_ck, w_cv, w_co, g_final):
    depth, d = g_mix.shape
    vec = lambda a: a[:, None, :]

    o_ra = C_QB
    w_main = jnp.concatenate(
        [w_in_even[:, :, :o_ra], w_in_even[:, :, o_ra + GLA_RANK:], w_in_even[:, :, o_ra:o_ra + GLA_RANK],
         jnp.zeros(w_in_even.shape[:2] + (RA_PAD - GLA_RANK,), w_in_even.dtype)], axis=2).astype(BF16)
    w_al = jnp.concatenate(
        [w_alpha, jnp.zeros((w_alpha.shape[0], RA_PAD - GLA_RANK, GLA_KW), w_alpha.dtype)], axis=1).astype(BF16)
    stacks = dict(
        g_mix=vec(g_mix), g_cross=vec(g_cross), w_cq=w_cq.astype(BF16), w_co=w_co.astype(BF16),
        w_in_e=w_main, w_alpha=w_al, b_alpha=vec(b_alpha), g_gla=vec(g_gla_out), w_out_e=w_out_even.astype(BF16),
        w_in_o=w_in_odd.astype(BF16), g_sgu=vec(g_sgu_v), b_sgu=vec(b_sgu_v), w_sp=w_sp,
        b_sp_t=b_sp.transpose(0, 2, 1), w_out_o=w_out_odd.astype(BF16))
    even_w, odd_w = {}, {}
    for l in range(depth):
        i = l // 2
        shared = {k: (stacks[k], l) for k in ("g_mix", "g_cross", "w_cq", "w_co")}
        if l % 2 == 0:
            even_w[l] = dict(shared, w_in=(stacks["w_in_e"], i), w_alpha=(stacks["w_alpha"], i),
                             b_alpha=(stacks["b_alpha"], i), g_gla=(stacks["g_gla"], i),
                             w_out=(stacks["w_out_e"], i))
        else:
            odd_w[l] = dict(shared, w_in=(stacks["w_in_o"], i), g_sgu=(stacks["g_sgu"], i),
                            b_sgu=(stacks["b_sgu"], i), w_sp=(stacks["w_sp"], i),
                            b_sp_t=(stacks["b_sp_t"], i), w_out=(stacks["w_out_o"], i))

    mem_k_p, mem_v_p, mem_k_out, mem_v_out = _memkv(mem_prompt, g_mem, w_ck.astype(BF16), w_cv.astype(BF16))
    gfin = g_final.reshape(1, 1, -1)

    y_p, sbk_p, sbv_p, gla_p, _ = _run_group(
        x_prompt, None, None, None, mem_k_p, mem_v_p, even_w, odd_w, gfin, False)
    y_s, sbk_s, sbv_s, gla_s, sgu_s = _run_group(
        x_sample, cache_sb_k, cache_sb_v, state_gla, cache_mem_k, cache_mem_v, even_w, odd_w, gfin, True)

    return (y_p, y_s, sbk_p, sbv_p, gla_p, mem_k_out, mem_v_out, sbk_s, sbv_s, gla_s, sgu_s)
```

```python
import functools

import jax
import jax.numpy as jnp
from jax import lax
from jax.experimental import pallas as pl
from jax.experimental.pallas import tpu as pltpu

F32 = jnp.float32
BF16 = jnp.bfloat16

EPS = 1e-6
GLA_HEADS = 4
GLA_DK = 64
GLA_DV = 128
GLA_RANK = 16
GLA_TAU = 16.0
GLA_CHUNK = 64
GLA_KW = GLA_HEADS * GLA_DK
GLA_VW = GLA_HEADS * GLA_DV
SB_HEADS = 8
SB_DIM = 64
SB_W = SB_HEADS * SB_DIM
SB_KEYS = 128
SGU_GROUPS = 4
SGU_CHUNK = 128
X_HEADS = 4
LANES = 128
RA_PAD = LANES
VMEM_LIMIT = 56 * 1024 * 1024
LOG2_E = 1.4426950408889634
SB_MASKED_SCORE = -1e30
CARRY = object()
SB_UNDERFLOW_LOG2 = 160.0
GATE_SLABS = 2
GATE_GAP = 6
PAST_SLAB = 256

C_QA, C_KA, C_VA, C_GA = 0, 256, 512, 1024
C_QB, C_KB, C_VB, C_GB, C_RA, C_END = 1536, 2048, 2560, 3072, 3584, 3712


def _dot(a, b):
    return jnp.dot(a, b, preferred_element_type=F32)


def _dot_nt(a, b):
    return lax.dot_general(a, b, (((1,), (1,)), ((), ())), preferred_element_type=F32)


def _split2(a):
    hi = a.astype(BF16)
    lo = (a - hi.astype(F32)).astype(BF16)
    return hi, lo


def _rms(x, g):
    return x * lax.rsqrt(jnp.mean(x * x, axis=-1, keepdims=True) + EPS) * g


def _sigmoid(x):
    return 1.0 / (1.0 + jnp.exp(-x))


def _silu(x):
    return x * _sigmoid(x)


def _gelu(x):
    return x * (0.5 * (1.0 + jnp.tanh(0.7978845608028654 * (x + 0.044715 * (x * x * x)))))


def _softplus_neg_abs(z):
    return jnp.log1p(jnp.exp(-jnp.abs(z)))


def _iota(shape, dim):
    return lax.broadcasted_iota(jnp.int32, shape, dim)


def _cross_attend(x, g_cross, wq_ref, wo_ref, mkb_ref, mvb_ref):
    d = x.shape[-1]
    hd = d // X_HEADS
    hc = _rms(x, g_cross).astype(BF16)
    q = _dot(hc, wq_ref[...]) * (hd ** -0.5)
    outs = []
    for h in range(X_HEADS):
        sl = slice(h * hd, (h + 1) * hd)
        s = _dot_nt(q[:, sl].astype(BF16), mkb_ref[:, sl])
        e = jnp.exp(s - jnp.max(s, axis=-1, keepdims=True))
        l = jnp.sum(e, axis=-1, keepdims=True)
        outs.append(_dot(e.astype(BF16), mvb_ref[:, sl]) / l)
    o = jnp.concatenate(outs, axis=-1).astype(BF16)
    return x + _dot(o, wo_ref[...])


def _memkv_kernel(m_ref, g_ref, wk_ref, wv_ref, k_ref, v_ref, kh_ref, vh_ref):
    rows, n, d = m_ref.shape
    m = _rms(m_ref[...].reshape(rows * n, d), g_ref[0]).astype(BF16)
    k = _dot(m, wk_ref[0])
    v = _dot(m, wv_ref[0])
    hd = d // X_HEADS
    for r in range(rows):
        kr, vr = k[r * n:(r + 1) * n], v[r * n:(r + 1) * n]
        k_ref[0, r] = kr
        v_ref[0, r] = vr
        for h in range(X_HEADS):
            kh_ref[0, r, :, h, :] = kr[:, h * hd:(h + 1) * hd]
            vh_ref[0, r, :, h, :] = vr[:, h * hd:(h + 1) * hd]


def _memkv(mem, g_mem, w_ck, w_cv):
    b, n, d = mem.shape
    depth = g_mem.shape[0]
    hd = d // X_HEADS
    flat = jax.ShapeDtypeStruct((depth, b, n, d), F32)
    heads = jax.ShapeDtypeStruct((depth, b, n, X_HEADS, hd), F32)
    wspec = pl.BlockSpec((1, d, d), lambda l, i: (l, 0, 0))
    rows = MEM_ROWS if b % MEM_ROWS == 0 else 1
    fspec = pl.BlockSpec((1, rows, n, d), lambda l, i: (l, i, 0, 0))
    hspec = pl.BlockSpec((1, rows, n, X_HEADS, hd), lambda l, i: (l, i, 0, 0, 0))
    return pl.pallas_call(
        _memkv_kernel,
        grid=(depth, b // rows),
        in_specs=[pl.BlockSpec((rows, n, d), lambda l, i: (i, 0, 0)),
                  pl.BlockSpec((1, 1, d), lambda l, i: (l, 0, 0)),
                  wspec, wspec],
        out_specs=[fspec, fspec, hspec, hspec],
        out_shape=[flat, flat, heads, heads],
        compiler_params=pltpu.CompilerParams(
            dimension_semantics=("arbitrary", "arbitrary"), vmem_limit_bytes=VMEM_LIMIT),
        name="mem_kv",
    )(mem, g_mem.reshape(depth, 1, d), w_ck, w_cv)


def _even_kernel(*refs, tq, nt_static, past, has_state, n_prev, kv_major):
    it = iter(refs)
    x_ref = next(it)
    pk_ref = next(it) if past else None
    pv_ref = next(it) if past else None
    s0_ref = next(it) if has_state else None
    pko_ref = next(it) if n_prev else None
    pvo_ref = next(it) if n_prev else None
    mk_ref, mv_ref = next(it), next(it)
    gmix_ref, wa_ref, wb_ref, wr_ref, wal_ref, bal_ref, ggla_ref, wout_ref, gcr_ref, wq_ref, wo_ref = (
        next(it).at[0] for _ in range(11))
    xo_ref, ko_ref, vo_ref, so_ref = next(it), next(it), next(it), next(it)
    kscr, vscr, mkb, mvb, sst, qs, acc, car = (next(it) for _ in range(8))

    t = pl.program_id(1)
    nt = pl.num_programs(1)
    hist = kscr.shape[1]
    seq = tq * nt_static

    @pl.when(t == 0)
    def _init():
        mkb[...] = mk_ref[0, 0].astype(BF16)
        mvb[...] = mv_ref[0, 0].astype(BF16)
        half0 = (_iota((1, SB_W), 1) % LANES) < SB_DIM
        for c0 in range(0, past, PAST_SLAB):
            c1 = min(c0 + PAST_SLAB, past)
            pk = pk_ref[0, 0, :, c0:c1].T
            pv = pv_ref[0, 0, :, c0:c1].T
            for e in range(2):
                keep_e = half0 if e == 0 else jnp.logical_not(half0)
                kscr[e, c0:c1, :] = jnp.where(keep_e, pk, 0.0).astype(BF16)
                vscr[e, c0:c1, :] = jnp.where(keep_e, pv, 0.0).astype(BF16)
        for e in range(2):
            if hist > past + seq:
                kscr[e, past + seq:hist, :] = jnp.zeros((hist - past - seq, SB_W), BF16)
                vscr[e, past + seq:hist, :] = jnp.zeros((hist - past - seq, SB_W), BF16)
        if has_state:
            for h in range(GLA_HEADS):
                sst[h * GLA_DK:(h + 1) * GLA_DK, :] = s0_ref[0, 0, h]
        else:
            sst[...] = jnp.zeros(sst.shape, F32)

    x = x_ref[0]
    h = _rms(x, gmix_ref[...]).astype(BF16)

    def proj(c0, c1):
        for ref, lo, hi in ((wa_ref, 0, C_QB), (wb_ref, C_QB, C_RA), (wr_ref, C_RA, C_END)):
            if lo <= c0 and c1 <= hi:
                return _dot(h, ref[:, c0 - lo:c1 - lo])
        raise ValueError((c0, c1))

    qb = proj(C_QB, C_KB) * (SB_DIM ** -0.5 * LOG2_E)
    kb = proj(C_KB, C_VB)
    vb = proj(C_VB, C_GB)
    gla_out = []

    def gla_steps():
        ra = proj(C_RA, C_END).astype(BF16)
        yield
        qa = proj(C_QA, C_KA) * (GLA_DK ** -0.5)
        yield
        ka = proj(C_KA, C_VA)
        yield
        va = proj(C_VA, C_GA)
        yield
        za = _dot(ra, wal_ref[...]) + bal_ref[...]
        log_a = (jnp.minimum(za, 0.0) - _softplus_neg_abs(za)) * (1.0 / GLA_TAU)

        cl = GLA_CHUNK
        nc = tq // cl
        row_t = _iota((tq, tq), 0)
        col_t = _iota((tq, tq), 1)
        causal = ((row_t // cl) == (col_t // cl)) & (col_t <= row_t)
        ltri = jnp.where(causal, 1.0, 0.0).astype(BF16)
        g_hi, g_lo = _split2(log_a)
        b = _dot(jnp.concatenate([ltri, ltri], axis=1),
                 jnp.concatenate([g_hi, g_lo], axis=0))
        yield
        bt = b.T
        lane_t = _iota((1, tq), 1)
        b_last = [bt[:, (ci + 1) * cl - 1:(ci + 1) * cl] for ci in range(nc)]
        bl = jnp.broadcast_to(b_last[0], bt.shape)
        for ci in range(1, nc):
            bl = jnp.where(lane_t >= ci * cl, b_last[ci], bl)
        qeb = (qa * jnp.exp(b)).astype(BF16)
        ke = (ka * jnp.exp(-b)).astype(BF16)
        kdt = (ka.T * jnp.exp(bl - bt)).astype(BF16)
        vab = va.astype(BF16)
        yield
        lane_kw = _iota((1, GLA_KW), 1)
        zero_b = jnp.zeros((), BF16)
        qeh, o_intra, upd = [], [], []
        for hh in range(GLA_HEADS):
            ks = slice(hh * GLA_DK, (hh + 1) * GLA_DK)
            vs = slice(hh * GLA_DV, (hh + 1) * GLA_DV)
            q_h = jnp.where((lane_kw >= hh * GLA_DK) & (lane_kw < (hh + 1) * GLA_DK), qeb, zero_b)
            qeh.append(q_h)
            att = jnp.where(causal, _dot_nt(q_h, ke), 0.0).astype(BF16)
            o_intra.append(_dot(att, vab[:, vs]))
            yield
            kd_h = kdt[ks]
            kd_c = [jnp.where((lane_t >= ci * cl) & (lane_t < (ci + 1) * cl), kd_h, zero_b) for ci in range(nc)]
            upd.append(_dot(jnp.concatenate(kd_c, axis=0) if nc > 1 else kd_c[0], vab[:, vs]))
            yield
        s_all = sst[...]
        o_inter = []
        for ci in range(nc):
            rs = slice(ci * cl, (ci + 1) * cl)
            q_c = jnp.concatenate([qeh[hh][rs] for hh in range(GLA_HEADS)], axis=0)
            o_inter.append(_dot(q_c, s_all.astype(BF16)))
            u_c = jnp.concatenate([upd[hh][ci * GLA_DK:(ci + 1) * GLA_DK] for hh in range(GLA_HEADS)], axis=0)
            s_all = jnp.exp(b_last[ci]) * s_all + u_c
            yield
        sst[...] = s_all
        o_heads = []
        for hh in range(GLA_HEADS):
            inter = [o_inter[ci][hh * cl:(hh + 1) * cl] for ci in range(nc)]
            o_heads.append(o_intra[hh] + (jnp.concatenate(inter, axis=0) if nc > 1 else inter[0]))
        oa = jnp.concatenate(o_heads, axis=-1)
        ggla = ggla_ref[...]
        oa_n = []
        for hh in range(GLA_HEADS):
            vs = slice(hh * GLA_DV, (hh + 1) * GLA_DV)
            oa_n.append(_rms(oa[:, vs], ggla[:, vs]))
        gla_out.append(jnp.concatenate(oa_n, axis=-1))


    if kv_major:
        for j in range(n_prev):
            ko_ref[j, 0] = pko_ref[j, 0]
            vo_ref[j, 0] = pvo_ref[j, 0]
        ko_ref[n_prev, 0] = kb.T
        vo_ref[n_prev, 0] = vb.T
    else:
        ko_ref[0] = kb
        vo_ref[0] = vb
    half = (_iota((1, SB_W), 1) % LANES) < SB_DIM
    row0 = pl.multiple_of(past + t * tq, min(tq, SB_KEYS))
    kbb, vbb = kb.astype(BF16), vb.astype(BF16)
    zero_bf = jnp.zeros((), BF16)
    for e in range(2):
        keep_e = half if e == 0 else jnp.logical_not(half)
        kscr[e, pl.ds(row0, tq), :] = jnp.where(keep_e, kbb, zero_bf)
        vscr[e, pl.ds(row0, tq), :] = jnp.where(keep_e, vbb, zero_bf)
    qs[...] = qb.astype(BF16)
    acc[...] = jnp.zeros(acc.shape, F32)
    car[...] = jnp.zeros(car.shape, F32)

    ue_r = _iota((2 * SB_KEYS, 2 * LANES), 0) % SB_KEYS
    ue_c = _iota((2 * SB_KEYS, 2 * LANES), 1)
    ue2 = jnp.where((ue_c < LANES) | (ue_r >= ue_c - LANES), 1.0, 0.0).astype(BF16)
    n_pairs = SB_HEADS // 2

    def sb_block_steps(r0, rs, masked, gate=None):
        nr = tq - rs
        keys = pl.ds(r0, SB_KEYS)
        if masked:
            vis2 = (_iota((nr, 2 * SB_KEYS), 1) % SB_KEYS) < _iota((nr, 2 * SB_KEYS), 0)
        zs, lhs = [], []
        for p in range(n_pairs):
            ls_ = slice(p * LANES, (p + 1) * LANES)
            k2 = jnp.concatenate([kscr[0, keys, ls_], kscr[1, keys, ls_]], axis=0)
            z = _dot_nt(qs[rs:tq, ls_], k2)
            if masked:
                z = jnp.where(vis2, z, SB_MASKED_SCORE)
            sp = jnp.maximum(z, 0.0) + jnp.log2(1.0 + jnp.exp2(-jnp.abs(z)))
            zs.append(z)
            hi, lo = _split2(sp)
            for e in range(2):
                es = slice(e * SB_KEYS, (e + 1) * SB_KEYS)
                lhs.append(jnp.concatenate([hi[:, es], lo[:, es]], axis=1))
            yield
        sc = _dot(jnp.concatenate(lhs, axis=0), ue2)
        yield CARRY
        for p in range(n_pairs):
            ls_ = slice(p * LANES, (p + 1) * LANES)
            ws = []
            for e in range(2):
                hh = 2 * p + e
                sc_h = sc[hh * nr:(hh + 1) * nr]
                carry = car[hh, rs:tq, :]
                w = jnp.exp2(zs[p][:, e * SB_KEYS:(e + 1) * SB_KEYS] - sc_h[:, LANES:] - carry)
                ws.append(w.astype(BF16))
                car[hh, rs:tq, :] = carry + sc_h[:, :LANES]
            v2 = jnp.concatenate([vscr[0, keys, ls_], vscr[1, keys, ls_]], axis=0)
            if gate is not None:
                v2 = v2 * gate
            acc[p, rs:tq, :] += _dot(jnp.concatenate(ws, axis=1), v2)
            yield

    def pipelined(blocks):
        def head(g):
            for tok in g:
                if tok is CARRY:
                    return
                yield
        blocks = list(blocks)
        yield from head(blocks[0])
        for k, g in enumerate(blocks):
            if k + 1 < len(blocks):
                yield from head(blocks[k + 1])
            yield from g

    def interleave(streams):
        streams = list(streams)
        while streams:
            for g in list(streams):
                try:
                    next(g)
                except StopIteration:
                    streams.remove(g)

    nblk = (past + t * tq) // SB_KEYS
    step = 2 if (past % (2 * SB_KEYS) == 0 and tq % (2 * SB_KEYS) == 0) else 1

    def below_block(j, gate=None):
        return sb_block_steps(pl.multiple_of(j * SB_KEYS, SB_KEYS), 0, False, gate)

    gate = None if past >= step * SB_KEYS else (nblk >= step).astype(BF16)
    diag = [sb_block_steps(pl.multiple_of(row0 + dblk * SB_KEYS, SB_KEYS), dblk * SB_KEYS, True)
            for dblk in reversed(range(pl.cdiv(tq, SB_KEYS)))]
    first = [below_block(jnp.maximum(nblk - 1 - u, 0), gate) for u in range(step)]
    gates = []

    def gate_steps():
        for c0 in (C_GA, C_GB):
            for k in range(GATE_SLABS):
                for _ in range(GATE_GAP):
                    yield
                w = (C_QB - C_GA) // GATE_SLABS
                gates.append(proj(c0 + k * w, c0 + (k + 1) * w))
                yield

    interleave([pipelined(diag + first), gla_steps(), gate_steps()])
    ga = jnp.concatenate(gates[:GATE_SLABS], axis=-1)
    gb = jnp.concatenate(gates[GATE_SLABS:], axis=-1)
    oa = gla_out[0] * _silu(ga)

    @pl.when(t == nt - 1)
    def _emit_state():
        for hh in range(GLA_HEADS):
            so_ref[0, hh] = sst[hh * GLA_DK:(hh + 1) * GLA_DK, :]

    def still_live():
        lowest = car[0]
        for hh in range(1, SB_HEADS):
            lowest = jnp.minimum(lowest, car[hh])
        return (jnp.min(lowest) < SB_UNDERFLOW_LOG2).astype(jnp.int32)

    def more(c):
        return jnp.logical_and(c[0] < nblk, c[1] > 0)

    def below(c):
        i = c[0]
        interleave([pipelined([below_block(nblk - 1 - i - u) for u in range(step)])])
        return i + step, still_live()

    lax.while_loop(more, below, (jnp.int32(step), still_live()))

    ob = jnp.concatenate([acc[p] for p in range(n_pairs)], axis=-1)
    ob = ob * _silu(gb)

    mix = jnp.concatenate([oa, ob], axis=-1).astype(BF16)
    x = x + _dot(mix, wout_ref[...])
    xo_ref[0] = _cross_attend(x, gcr_ref[...], wq_ref, wo_ref, mkb, mvb)


def _const_spec(stacked, index):
    tail = stacked.shape[1:]
    return pl.BlockSpec((1,) + tail, lambda *_: (index,) + (0,) * len(tail), pipeline_mode=pl.Buffered(1))


def _even_layer(x, past_k, past_v, s0, prev_kv, mem_k, mem_v, layer, wts, tq):
    bsz, t, d = x.shape
    past = 0 if past_k is None else past_k.shape[-1]
    has_state = s0 is not None
    nmem = mem_k.shape[2]
    li = layer // 2
    kv_major = tq % LANES == 0
    n_prev = 0 if prev_kv is None else prev_kv[0].shape[0]
    assert kv_major or prev_kv is None
    assert t % tq == 0 and tq % GLA_CHUNK == 0 and past % SB_KEYS == 0
    assert (tq % SB_KEYS == 0 or t == tq) and SB_KEYS == LANES

    tile = lambda b, i: (b, i, 0)
    args, specs = [x], [pl.BlockSpec((1, tq, d), tile)]
    if past:
        args += [past_k, past_v]
        specs += [pl.BlockSpec((1, 1, SB_W, past), lambda b, i: (li, b, 0, 0))] * 2
    if has_state:
        args.append(s0)
        specs.append(pl.BlockSpec((1, 1, GLA_HEADS, GLA_DK, GLA_DV), lambda b, i: (li, b, 0, 0, 0)))
    if n_prev:
        args += list(prev_kv)
        specs += [pl.BlockSpec((n_prev, 1, SB_W, tq), lambda b, i: (0, b, 0, i))] * 2
    args += [mem_k, mem_v]
    specs += [pl.BlockSpec((1, 1, nmem, d), lambda b, i: (layer, b, 0, 0))] * 2
    for name in ("g_mix", "w_in_a", "w_in_b", "w_in_r", "w_alpha", "b_alpha", "g_gla", "w_out", "g_cross", "w_cq", "w_co"):
        stacked, index = wts[name]
        args.append(stacked)
        specs.append(_const_spec(stacked, index))

    if kv_major:
        kv_shape = jax.ShapeDtypeStruct((n_prev + 1, bsz, SB_W, t), F32)
        kv_spec = pl.BlockSpec((n_prev + 1, 1, SB_W, tq), lambda b, i: (0, b, 0, i))
    else:
        kv_shape = jax.ShapeDtypeStruct((bsz, t, SB_W), F32)
        kv_spec = pl.BlockSpec((1, tq, SB_W), tile)
    out_shape = [jax.ShapeDtypeStruct((bsz, t, d), F32), kv_shape, kv_shape,
                 jax.ShapeDtypeStruct((bsz, GLA_HEADS, GLA_DK, GLA_DV), F32)]
    out_specs = [pl.BlockSpec((1, tq, d), tile), kv_spec, kv_spec,
                 pl.BlockSpec((1, GLA_HEADS, GLA_DK, GLA_DV), lambda b, i: (b, 0, 0, 0))]
    hist = past + pl.cdiv(t, SB_KEYS) * SB_KEYS
    scratch = [pltpu.VMEM((2, hist, SB_W), BF16), pltpu.VMEM((2, hist, SB_W), BF16),
               pltpu.VMEM((nmem, d), BF16), pltpu.VMEM((nmem, d), BF16),
               pltpu.VMEM((GLA_KW, GLA_DV), F32),
               pltpu.VMEM((tq, SB_W), BF16),
               pltpu.VMEM((SB_HEADS // 2, tq, LANES), F32),
               pltpu.VMEM((SB_HEADS, tq, LANES), F32)]
    return pl.pallas_call(
        functools.partial(_even_kernel, tq=tq, nt_static=t // tq, past=past, has_state=has_state,
                          n_prev=n_prev, kv_major=kv_major),
        grid=(bsz, t // tq),
        in_specs=specs, out_specs=out_specs, out_shape=out_shape, scratch_shapes=scratch,
        compiler_params=pltpu.CompilerParams(
            dimension_semantics=("arbitrary", "arbitrary"), vmem_limit_bytes=VMEM_LIMIT),
        name="even_layer",
    )(*args)


def _odd_kernel(*refs, tq, cl, emit_vn, final):
    it = iter(refs)
    x_ref, mk_ref, mv_ref = next(it), next(it), next(it)
    gmix_ref, w_ref, gv_ref, bv_ref, wsp_ref, bsp_ref, wout_ref, gcr_ref, wq_ref, wo_ref = (
        next(it).at[0] for _ in range(10))
    gfin_ref = next(it).at[0] if final else None
    xo_ref = next(it)
    vn_ref = next(it) if emit_vn else None
    mkb, mvb = next(it), next(it)

    @pl.when(pl.program_id(1) == 0)
    def _init():
        mkb[...] = mk_ref[0, 0].astype(BF16)
        mvb[...] = mv_ref[0, 0].astype(BF16)

    x = x_ref[0]
    d = x.shape[-1]
    h = _rms(x, gmix_ref[...]).astype(BF16)
    u = _dot(h, w_ref[:, 0:d])
    v = _dot(h, w_ref[:, d:2 * d])
    g = _dot(h, w_ref[:, 2 * d:3 * d])

    gv = _gelu(v)
    xc = gv - jnp.mean(gv, axis=-1, keepdims=True)
    v_n = xc * lax.rsqrt(jnp.mean(xc * xc, axis=-1, keepdims=True) + EPS) * gv_ref[...] + bv_ref[...]
    if emit_vn:
        vn_ref[0] = v_n

    gw = d // SGU_GROUPS
    keep = _iota((cl, cl), 1) <= _iota((cl, cl), 0)
    vnb = v_n.astype(BF16)
    bsp = bsp_ref[0:cl, :]
    wgs = [jnp.where(keep, wsp_ref[gi, 0:cl, 0:cl], 0.0).astype(BF16) for gi in range(SGU_GROUPS)]
    rows = []
    for c in range(tq // cl):
        cols = []
        for gi in range(SGU_GROUPS):
            cols.append(_dot(wgs[gi], vnb[c * cl:(c + 1) * cl, gi * gw:(gi + 1) * gw]) + bsp[:, gi:gi + 1])
        rows.append(jnp.concatenate(cols, axis=-1))
    s = jnp.concatenate(rows, axis=0) if len(rows) > 1 else rows[0]

    y = (_gelu(u) * s * _silu(g)).astype(BF16)
    x = x + _dot(y, wout_ref[...])
    x = _cross_attend(x, gcr_ref[...], wq_ref, wo_ref, mkb, mvb)
    if final:
        x = _rms(x, gfin_ref[...])
    xo_ref[0] = x


def _odd_layer(x, mem_k, mem_v, layer, wts, tq, emit_vn, g_final):
    bsz, t, d = x.shape
    nmem = mem_k.shape[2]
    cl = min(SGU_CHUNK, t)
    final = g_final is not None
    assert t % tq == 0 and tq % cl == 0

    tile = lambda b, i: (b, i, 0)
    args = [x, mem_k, mem_v]
    specs = [pl.BlockSpec((1, tq, d), tile)]
    specs += [pl.BlockSpec((1, 1, nmem, d), lambda b, i: (layer, b, 0, 0))] * 2
    consts = [wts[name] for name in
              ("g_mix", "w_in", "g_sgu", "b_sgu", "w_sp", "b_sp_t", "w_out", "g_cross", "w_cq", "w_co")]
    if final:
        consts.append((g_final, 0))
    for stacked, index in consts:
        args.append(stacked)
        specs.append(_const_spec(stacked, index))

    out_shape = [jax.ShapeDtypeStruct((bsz, t, d), F32)]
    out_specs = [pl.BlockSpec((1, tq, d), tile)]
    if emit_vn:
        out_shape.append(jax.ShapeDtypeStruct((bsz, t, d), F32))
        out_specs.append(pl.BlockSpec((1, tq, d), tile))
    return pl.pallas_call(
        functools.partial(_odd_kernel, tq=tq, cl=cl, emit_vn=emit_vn, final=final),
        grid=(bsz, t // tq),
        in_specs=specs, out_specs=out_specs, out_shape=out_shape,
        scratch_shapes=[pltpu.VMEM((nmem, d), BF16), pltpu.VMEM((nmem, d), BF16)],
        compiler_params=pltpu.CompilerParams(
            dimension_semantics=("arbitrary", "arbitrary"), vmem_limit_bytes=VMEM_LIMIT),
        name="odd_layer",
    )(*args)


EVEN_TILE = 256
MEM_ROWS = 2
ODD_TILE = 1024


def _tile_rows(t, layer):
    return min(EVEN_TILE if layer % 2 == 0 else ODD_TILE, t)


def _run_group(x, past_k, past_v, s0, mem_k, mem_v, even_w, odd_w, g_final, emit_vn):
    depth = mem_k.shape[0]
    bsz, t, d = x.shape
    tq = _tile_rows(t, 0)
    mem_k = mem_k.reshape(depth, bsz, -1, d)
    mem_v = mem_v.reshape(depth, bsz, -1, d)
    if past_k is not None:
        n_even, _, past = past_k.shape[:3]
        past_k = past_k.transpose(0, 1, 3, 4, 2).reshape(n_even, bsz, SB_W, past)
        past_v = past_v.transpose(0, 1, 3, 4, 2).reshape(n_even, bsz, SB_W, past)
    kv_major = tq % LANES == 0
    sb_k, sb_v, gla_s, sgu_v = [], [], [], []
    prev_kv = None
    for l in range(depth):
        if l % 2 == 0:
            x, kb, vb, s_new = _even_layer(x, past_k, past_v, s0, prev_kv, mem_k, mem_v, l, even_w[l], tq)
            if kv_major:
                prev_kv = (kb, vb)
            else:
                sb_k.append(kb.reshape(bsz, t, SB_HEADS, SB_DIM))
                sb_v.append(vb.reshape(bsz, t, SB_HEADS, SB_DIM))
            gla_s.append(s_new)
        else:
            outs = _odd_layer(x, mem_k, mem_v, l, odd_w[l], _tile_rows(t, l), emit_vn,
                              g_final if l == depth - 1 else None)
            x = outs[0]
            if emit_vn:
                sgu_v.append(outs[1])
    if kv_major:
        sb_k, sb_v = (a.reshape(a.shape[0], bsz, SB_HEADS, SB_DIM, t).transpose(0, 1, 4, 2, 3) for a in prev_kv)
    else:
        sb_k, sb_v = jnp.stack(sb_k), jnp.stack(sb_v)
    return x, sb_k, sb_v, jnp.stack(gla_s), (jnp.stack(sgu_v) if emit_vn else None)


def kernel(x_prompt, x_sample, cache_sb_k, cache_sb_v, state_gla, cache_mem_k, cache_mem_v, mem_prompt, g_mix, w_in_even, w_alpha, b_alpha, g_gla_out, w_out_even, w_in_odd, g_sgu_v, b_sgu_v, w_sp, b_sp, w_out_odd, g_cross, g_mem, w_cq, w_ck, w_cv, w_co, g_final):
    depth, d = g_mix.shape
    vec = lambda a: a[:, None, :]

    o_ra = C_QB
    w_in_r = jnp.concatenate(
        [w_in_even[:, :, o_ra:o_ra + GLA_RANK],
         jnp.zeros(w_in_even.shape[:2] + (RA_PAD - GLA_RANK,), w_in_even.dtype)], axis=2).astype(BF16)
    w_al = jnp.concatenate(
        [w_alpha, jnp.zeros((w_alpha.shape[0], RA_PAD - GLA_RANK, GLA_KW), w_alpha.dtype)], axis=1).astype(BF16)
    stacks = dict(
        g_mix=vec(g_mix), g_cross=vec(g_cross), w_cq=w_cq.astype(BF16), w_co=w_co.astype(BF16),
        w_in_a=w_in_even[:, :, :o_ra].astype(BF16), w_in_b=w_in_even[:, :, o_ra + GLA_RANK:].astype(BF16),
        w_in_r=w_in_r, w_alpha=w_al, b_alpha=vec(b_alpha), g_gla=vec(g_gla_out), w_out_e=w_out_even.astype(BF16),
        w_in_o=w_in_odd.astype(BF16), g_sgu=vec(g_sgu_v), b_sgu=vec(b_sgu_v), w_sp=w_sp,
        b_sp_t=b_sp.transpose(0, 2, 1), w_out_o=w_out_odd.astype(BF16))
    even_w, odd_w = {}, {}
    for l in range(depth):
        i = l // 2
        shared = {k: (stacks[k], l) for k in ("g_mix", "g_cross", "w_cq", "w_co")}
        if l % 2 == 0:
            even_w[l] = dict(shared, w_in_a=(stacks["w_in_a"], i), w_in_b=(stacks["w_in_b"], i),
                             w_in_r=(stacks["w_in_r"], i), w_alpha=(stacks["w_alpha"], i),
                             b_alpha=(stacks["b_alpha"], i), g_gla=(stacks["g_gla"], i),
                             w_out=(stacks["w_out_e"], i))
        else:
            odd_w[l] = dict(shared, w_in=(stacks["w_in_o"], i), g_sgu=(stacks["g_sgu"], i),
                            b_sgu=(stacks["b_sgu"], i), w_sp=(stacks["w_sp"], i),
                            b_sp_t=(stacks["b_sp_t"], i), w_out=(stacks["w_out_o"], i))

    mem_k_p, mem_v_p, mem_k_out, mem_v_out = _memkv(mem_prompt, g_mem, w_ck.astype(BF16), w_cv.astype(BF16))
    gfin = g_final.reshape(1, 1, -1)

    y_p, sbk_p, sbv_p, gla_p, _ = _run_group(
        x_prompt, None, None, None, mem_k_p, mem_v_p, even_w, odd_w, gfin, False)
    y_s, sbk_s, sbv_s, gla_s, sgu_s = _run_group(
        x_sample, cache_sb_k, cache_sb_v, state_gla, cache_mem_k, cache_mem_v, even_w, odd_w, gfin, True)

    return (y_p, y_s, sbk_p, sbv_p, gla_p, mem_k_out, mem_v_out, sbk_s, sbv_s, gla_s, sgu_s)
```

```python
import functools

import jax
import jax.numpy as jnp
from jax import lax
from jax.experimental import pallas as pl
from jax.experimental.pallas import tpu as pltpu

F32 = jnp.float32
BF16 = jnp.bfloat16

EPS = 1e-6
GLA_HEADS = 4
GLA_DK = 64
GLA_DV = 128
GLA_RANK = 16
GLA_TAU = 16.0
GLA_CHUNK = 64
GLA_KW = GLA_HEADS * GLA_DK
GLA_VW = GLA_HEADS * GLA_DV
SB_HEADS = 8
SB_DIM = 64
SB_W = SB_HEADS * SB_DIM
SB_KEYS = 128
SGU_GROUPS = 4
SGU_CHUNK = 128
X_HEADS = 4
LANES = 128
RA_PAD = LANES
VMEM_LIMIT = 56 * 1024 * 1024
LOG2_E = 1.4426950408889634
SB_MASKED_SCORE = -1e30
CARRY = object()
SB_UNDERFLOW_LOG2 = 160.0
GATE_SLABS = 2
GATE_GAP = 6
PAST_SLAB = 256

C_QA, C_KA, C_VA, C_GA = 0, 256, 512, 1024
C_QB, C_KB, C_VB, C_GB, C_RA, C_END = 1536, 2048, 2560, 3072, 3584, 3712


def _dot(a, b):
    return jnp.dot(a, b, preferred_element_type=F32)


def _dot_nt(a, b):
    return lax.dot_general(a, b, (((1,), (1,)), ((), ())), preferred_element_type=F32)


def _split2(a):
    hi = a.astype(BF16)
    lo = (a - hi.astype(F32)).astype(BF16)
    return hi, lo


def _rms(x, g):
    return x * lax.rsqrt(jnp.mean(x * x, axis=-1, keepdims=True) + EPS) * g


def _sigmoid(x):
    return 1.0 / (1.0 + jnp.exp(-x))


def _silu(x):
    return x * _sigmoid(x)


def _gelu(x):
    return x * (0.5 * (1.0 + jnp.tanh(0.7978845608028654 * (x + 0.044715 * (x * x * x)))))


def _softplus_neg_abs(z):
    return jnp.log1p(jnp.exp(-jnp.abs(z)))


def _iota(shape, dim):
    return lax.broadcasted_iota(jnp.int32, shape, dim)


def _cross_attend(x, g_cross, wq_ref, wo_ref, mems):
    d = x.shape[-1]
    hd = d // X_HEADS
    hc = _rms(x, g_cross).astype(BF16)
    q = (_dot(hc, wq_ref[...]) * (hd ** -0.5)).astype(BF16)
    n = x.shape[0] // len(mems)
    groups = []
    for r, (mkb_ref, mvb_ref) in enumerate(mems):
        outs = []
        for h in range(X_HEADS):
            sl = slice(h * hd, (h + 1) * hd)
            s = _dot_nt(q[r * n:(r + 1) * n, sl], mkb_ref[:, sl])
            e = jnp.exp(s - jnp.max(s, axis=-1, keepdims=True))
            l = jnp.sum(e, axis=-1, keepdims=True)
            outs.append(_dot(e.astype(BF16), mvb_ref[:, sl]) / l)
        groups.append(jnp.concatenate(outs, axis=-1))
    o = (jnp.concatenate(groups, axis=0) if len(groups) > 1 else groups[0]).astype(BF16)
    return x + _dot(o, wo_ref[...])


def _memkv_kernel(m_ref, g_ref, wk_ref, wv_ref, k_ref, v_ref, kh_ref, vh_ref):
    rows, n, d = m_ref.shape
    m = _rms(m_ref[...].reshape(rows * n, d), g_ref[0]).astype(BF16)
    k = _dot(m, wk_ref[0])
    v = _dot(m, wv_ref[0])
    hd = d // X_HEADS
    for r in range(rows):
        kr, vr = k[r * n:(r + 1) * n], v[r * n:(r + 1) * n]
        k_ref[0, r] = kr
        v_ref[0, r] = vr
        for h in range(X_HEADS):
            kh_ref[0, r, :, h, :] = kr[:, h * hd:(h + 1) * hd]
            vh_ref[0, r, :, h, :] = vr[:, h * hd:(h + 1) * hd]


def _memkv(mem, g_mem, w_ck, w_cv):
    b, n, d = mem.shape
    depth = g_mem.shape[0]
    hd = d // X_HEADS
    flat = jax.ShapeDtypeStruct((depth, b, n, d), F32)
    heads = jax.ShapeDtypeStruct((depth, b, n, X_HEADS, hd), F32)
    wspec = pl.BlockSpec((1, d, d), lambda l, i: (l, 0, 0))
    rows = MEM_ROWS if b % MEM_ROWS == 0 else 1
    fspec = pl.BlockSpec((1, rows, n, d), lambda l, i: (l, i, 0, 0))
    hspec = pl.BlockSpec((1, rows, n, X_HEADS, hd), lambda l, i: (l, i, 0, 0, 0))
    return pl.pallas_call(
        _memkv_kernel,
        grid=(depth, b // rows),
        in_specs=[pl.BlockSpec((rows, n, d), lambda l, i: (i, 0, 0)),
                  pl.BlockSpec((1, 1, d), lambda l, i: (l, 0, 0)),
                  wspec, wspec],
        out_specs=[fspec, fspec, hspec, hspec],
        out_shape=[flat, flat, heads, heads],
        compiler_params=pltpu.CompilerParams(
            dimension_semantics=("arbitrary", "arbitrary"), vmem_limit_bytes=VMEM_LIMIT),
        name="mem_kv",
    )(mem, g_mem.reshape(depth, 1, d), w_ck, w_cv)


def _even_kernel(*refs, tq, nt_static, past, has_state, n_prev, kv_major):
    it = iter(refs)
    x_ref = next(it)
    pk_ref = next(it) if past else None
    pv_ref = next(it) if past else None
    s0_ref = next(it) if has_state else None
    pko_ref = next(it) if n_prev else None
    pvo_ref = next(it) if n_prev else None
    mk_ref, mv_ref = next(it), next(it)
    gmix_ref, wa_ref, wb_ref, wr_ref, wal_ref, bal_ref, ggla_ref, wout_ref, gcr_ref, wq_ref, wo_ref = (
        next(it).at[0] for _ in range(11))
    xo_ref, ko_ref, vo_ref, so_ref = next(it), next(it), next(it), next(it)
    kscr, vscr, mkb, mvb, sst, qs, acc, car = (next(it) for _ in range(8))

    t = pl.program_id(1)
    nt = pl.num_programs(1)
    hist = kscr.shape[1]
    seq = tq * nt_static

    @pl.when(t == 0)
    def _init():
        mkb[...] = mk_ref[0, 0].astype(BF16)
        mvb[...] = mv_ref[0, 0].astype(BF16)
        half0 = (_iota((1, SB_W), 1) % LANES) < SB_DIM
        for c0 in range(0, past, PAST_SLAB):
            c1 = min(c0 + PAST_SLAB, past)
            pk = pk_ref[0, 0, :, c0:c1].T
            pv = pv_ref[0, 0, :, c0:c1].T
            for e in range(2):
                keep_e = half0 if e == 0 else jnp.logical_not(half0)
                kscr[e, c0:c1, :] = jnp.where(keep_e, pk, 0.0).astype(BF16)
                vscr[e, c0:c1, :] = jnp.where(keep_e, pv, 0.0).astype(BF16)
        for e in range(2):
            if hist > past + seq:
                kscr[e, past + seq:hist, :] = jnp.zeros((hist - past - seq, SB_W), BF16)
                vscr[e, past + seq:hist, :] = jnp.zeros((hist - past - seq, SB_W), BF16)
        if has_state:
            for h in range(GLA_HEADS):
                sst[h * GLA_DK:(h + 1) * GLA_DK, :] = s0_ref[0, 0, h]
        else:
            sst[...] = jnp.zeros(sst.shape, F32)

    x = x_ref[0]
    h = _rms(x, gmix_ref[...]).astype(BF16)

    def proj(c0, c1):
        for ref, lo, hi in ((wa_ref, 0, C_QB), (wb_ref, C_QB, C_RA), (wr_ref, C_RA, C_END)):
            if lo <= c0 and c1 <= hi:
                return _dot(h, ref[:, c0 - lo:c1 - lo])
        raise ValueError((c0, c1))

    qb = proj(C_QB, C_KB) * (SB_DIM ** -0.5 * LOG2_E)
    kb = proj(C_KB, C_VB)
    vb = proj(C_VB, C_GB)
    gla_out = []

    def gla_steps():
        ra = proj(C_RA, C_END).astype(BF16)
        yield
        qa = proj(C_QA, C_KA) * (GLA_DK ** -0.5)
        yield
        ka = proj(C_KA, C_VA)
        yield
        va = proj(C_VA, C_GA)
        yield
        za = _dot(ra, wal_ref[...]) + bal_ref[...]
        log_a = (jnp.minimum(za, 0.0) - _softplus_neg_abs(za)) * (1.0 / GLA_TAU)

        cl = GLA_CHUNK
        nc = tq // cl
        row_t = _iota((tq, tq), 0)
        col_t = _iota((tq, tq), 1)
        causal = ((row_t // cl) == (col_t // cl)) & (col_t <= row_t)
        ltri = jnp.where(causal, 1.0, 0.0).astype(BF16)
        g_hi, g_lo = _split2(log_a)
        b = _dot(jnp.concatenate([ltri, ltri], axis=1),
                 jnp.concatenate([g_hi, g_lo], axis=0))
        yield
        bt = b.T
        lane_t = _iota((1, tq), 1)
        b_last = [bt[:, (ci + 1) * cl - 1:(ci + 1) * cl] for ci in range(nc)]
        bl = jnp.broadcast_to(b_last[0], bt.shape)
        for ci in range(1, nc):
            bl = jnp.where(lane_t >= ci * cl, b_last[ci], bl)
        qeb = (qa * jnp.exp(b)).astype(BF16)
        ke = (ka * jnp.exp(-b)).astype(BF16)
        kdt = (ka.T * jnp.exp(bl - bt)).astype(BF16)
        vab = va.astype(BF16)
        yield
        lane_kw = _iota((1, GLA_KW), 1)
        zero_b = jnp.zeros((), BF16)
        qeh, o_intra, upd = [], [], []
        for hh in range(GLA_HEADS):
            ks = slice(hh * GLA_DK, (hh + 1) * GLA_DK)
            vs = slice(hh * GLA_DV, (hh + 1) * GLA_DV)
            q_h = jnp.where((lane_kw >= hh * GLA_DK) & (lane_kw < (hh + 1) * GLA_DK), qeb, zero_b)
            qeh.append(q_h)
            att = jnp.where(causal, _dot_nt(q_h, ke), 0.0).astype(BF16)
            o_intra.append(_dot(att, vab[:, vs]))
            yield
            kd_h = kdt[ks]
            kd_c = [jnp.where((lane_t >= ci * cl) & (lane_t < (ci + 1) * cl), kd_h, zero_b) for ci in range(nc)]
            upd.append(_dot(jnp.concatenate(kd_c, axis=0) if nc > 1 else kd_c[0], vab[:, vs]))
            yield
        s_all = sst[...]
        o_inter = []
        for ci in range(nc):
            rs = slice(ci * cl, (ci + 1) * cl)
            q_c = jnp.concatenate([qeh[hh][rs] for hh in range(GLA_HEADS)], axis=0)
            o_inter.append(_dot(q_c, s_all.astype(BF16)))
            u_c = jnp.concatenate([upd[hh][ci * GLA_DK:(ci + 1) * GLA_DK] for hh in range(GLA_HEADS)], axis=0)
            s_all = jnp.exp(b_last[ci]) * s_all + u_c
            yield
        sst[...] = s_all
        o_heads = []
        for hh in range(GLA_HEADS):
            inter = [o_inter[ci][hh * cl:(hh + 1) * cl] for ci in range(nc)]
            o_heads.append(o_intra[hh] + (jnp.concatenate(inter, axis=0) if nc > 1 else inter[0]))
        oa = jnp.concatenate(o_heads, axis=-1)
        ggla = ggla_ref[...]
        oa_n = []
        for hh in range(GLA_HEADS):
            vs = slice(hh * GLA_DV, (hh + 1) * GLA_DV)
            oa_n.append(_rms(oa[:, vs], ggla[:, vs]))
        gla_out.append(jnp.concatenate(oa_n, axis=-1))


    if kv_major:
        for j in range(n_prev):
            ko_ref[j, 0] = pko_ref[j, 0]
            vo_ref[j, 0] = pvo_ref[j, 0]
        ko_ref[n_prev, 0] = kb.T
        vo_ref[n_prev, 0] = vb.T
    else:
        ko_ref[0] = kb
        vo_ref[0] = vb
    half = (_iota((1, SB_W), 1) % LANES) < SB_DIM
    row0 = pl.multiple_of(past + t * tq, min(tq, SB_KEYS))
    kbb, vbb = kb.astype(BF16), vb.astype(BF16)
    zero_bf = jnp.zeros((), BF16)
    for e in range(2):
        keep_e = half if e == 0 else jnp.logical_not(half)
        kscr[e, pl.ds(row0, tq), :] = jnp.where(keep_e, kbb, zero_bf)
        vscr[e, pl.ds(row0, tq), :] = jnp.where(keep_e, vbb, zero_bf)
    qs[...] = qb.astype(BF16)
    acc[...] = jnp.zeros(acc.shape, F32)
    car[...] = jnp.zeros(car.shape, F32)

    ue_r = _iota((2 * SB_KEYS, 2 * LANES), 0) % SB_KEYS
    ue_c = _iota((2 * SB_KEYS, 2 * LANES), 1)
    ue2 = jnp.where((ue_c < LANES) | (ue_r >= ue_c - LANES), 1.0, 0.0).astype(BF16)
    n_pairs = SB_HEADS // 2

    def sb_block_steps(r0, rs, masked, gate=None):
        nr = tq - rs
        keys = pl.ds(r0, SB_KEYS)
        if masked:
            vis2 = (_iota((nr, 2 * SB_KEYS), 1) % SB_KEYS) < _iota((nr, 2 * SB_KEYS), 0)
        zs, lhs = [], []
        for p in range(n_pairs):
            ls_ = slice(p * LANES, (p + 1) * LANES)
            k2 = jnp.concatenate([kscr[0, keys, ls_], kscr[1, keys, ls_]], axis=0)
            z = _dot_nt(qs[rs:tq, ls_], k2)
            if masked:
                z = jnp.where(vis2, z, SB_MASKED_SCORE)
            sp = jnp.maximum(z, 0.0) + jnp.log2(1.0 + jnp.exp2(-jnp.abs(z)))
            zs.append(z)
            hi, lo = _split2(sp)
            for e in range(2):
                es = slice(e * SB_KEYS, (e + 1) * SB_KEYS)
                lhs.append(jnp.concatenate([hi[:, es], lo[:, es]], axis=1))
            yield
        sc = _dot(jnp.concatenate(lhs, axis=0), ue2)
        yield CARRY
        for p in range(n_pairs):
            ls_ = slice(p * LANES, (p + 1) * LANES)
            ws = []
            for e in range(2):
                hh = 2 * p + e
                sc_h = sc[hh * nr:(hh + 1) * nr]
                carry = car[hh, rs:tq, :]
                w = jnp.exp2(zs[p][:, e * SB_KEYS:(e + 1) * SB_KEYS] - sc_h[:, LANES:] - carry)
                ws.append(w.astype(BF16))
                car[hh, rs:tq, :] = carry + sc_h[:, :LANES]
            v2 = jnp.concatenate([vscr[0, keys, ls_], vscr[1, keys, ls_]], axis=0)
            if gate is not None:
                v2 = v2 * gate
            acc[p, rs:tq, :] += _dot(jnp.concatenate(ws, axis=1), v2)
            yield

    def pipelined(blocks):
        def head(g):
            for tok in g:
                if tok is CARRY:
                    return
                yield
        blocks = list(blocks)
        yield from head(blocks[0])
        for k, g in enumerate(blocks):
            if k + 1 < len(blocks):
                yield from head(blocks[k + 1])
            yield from g

    def interleave(streams):
        streams = list(streams)
        while streams:
            for g in list(streams):
                try:
                    next(g)
                except StopIteration:
                    streams.remove(g)

    nblk = (past + t * tq) // SB_KEYS
    step = 2 if (past % (2 * SB_KEYS) == 0 and tq % (2 * SB_KEYS) == 0) else 1

    def below_block(j, gate=None):
        return sb_block_steps(pl.multiple_of(j * SB_KEYS, SB_KEYS), 0, False, gate)

    gate = None if past >= step * SB_KEYS else (nblk >= step).astype(BF16)
    diag = [sb_block_steps(pl.multiple_of(row0 + dblk * SB_KEYS, SB_KEYS), dblk * SB_KEYS, True)
            for dblk in reversed(range(pl.cdiv(tq, SB_KEYS)))]
    first = [below_block(jnp.maximum(nblk - 1 - u, 0), gate) for u in range(step)]
    gates = []

    def gate_steps():
        for c0 in (C_GA, C_GB):
            for k in range(GATE_SLABS):
                for _ in range(GATE_GAP):
                    yield
                w = (C_QB - C_GA) // GATE_SLABS
                gates.append(proj(c0 + k * w, c0 + (k + 1) * w))
                yield

    interleave([pipelined(diag + first), gla_steps(), gate_steps()])
    ga = jnp.concatenate(gates[:GATE_SLABS], axis=-1)
    gb = jnp.concatenate(gates[GATE_SLABS:], axis=-1)
    oa = gla_out[0] * _silu(ga)

    @pl.when(t == nt - 1)
    def _emit_state():
        for hh in range(GLA_HEADS):
            so_ref[0, hh] = sst[hh * GLA_DK:(hh + 1) * GLA_DK, :]

    def still_live():
        lowest = car[0]
        for hh in range(1, SB_HEADS):
            lowest = jnp.minimum(lowest, car[hh])
        return (jnp.min(lowest) < SB_UNDERFLOW_LOG2).astype(jnp.int32)

    def more(c):
        return jnp.logical_and(c[0] < nblk, c[1] > 0)

    def below(c):
        i = c[0]
        interleave([pipelined([below_block(nblk - 1 - i - u) for u in range(step)])])
        return i + step, still_live()

    lax.while_loop(more, below, (jnp.int32(step), still_live()))

    ob = jnp.concatenate([acc[p] for p in range(n_pairs)], axis=-1)
    ob = ob * _silu(gb)

    mix = jnp.concatenate([oa, ob], axis=-1).astype(BF16)
    x = x + _dot(mix, wout_ref[...])
    xo_ref[0] = _cross_attend(x, gcr_ref[...], wq_ref, wo_ref, [(mkb, mvb)])


def _const_spec(stacked, index):
    tail = stacked.shape[1:]
    return pl.BlockSpec((1,) + tail, lambda *_: (index,) + (0,) * len(tail), pipeline_mode=pl.Buffered(1))


def _even_layer(x, past_k, past_v, s0, prev_kv, mem_k, mem_v, layer, wts, tq):
    bsz, t, d = x.shape
    past = 0 if past_k is None else past_k.shape[-1]
    has_state = s0 is not None
    nmem = mem_k.shape[2]
    li = layer // 2
    kv_major = tq % LANES == 0
    n_prev = 0 if prev_kv is None else prev_kv[0].shape[0]
    assert kv_major or prev_kv is None
    assert t % tq == 0 and tq % GLA_CHUNK == 0 and past % SB_KEYS == 0
    assert (tq % SB_KEYS == 0 or t == tq) and SB_KEYS == LANES

    tile = lambda b, i: (b, i, 0)
    args, specs = [x], [pl.BlockSpec((1, tq, d), tile)]
    if past:
        args += [past_k, past_v]
        specs += [pl.BlockSpec((1, 1, SB_W, past), lambda b, i: (li, b, 0, 0))] * 2
    if has_state:
        args.append(s0)
        specs.append(pl.BlockSpec((1, 1, GLA_HEADS, GLA_DK, GLA_DV), lambda b, i: (li, b, 0, 0, 0)))
    if n_prev:
        args += list(prev_kv)
        specs += [pl.BlockSpec((n_prev, 1, SB_W, tq), lambda b, i: (0, b, 0, i))] * 2
    args += [mem_k, mem_v]
    specs += [pl.BlockSpec((1, 1, nmem, d), lambda b, i: (layer, b, 0, 0))] * 2
    for name in ("g_mix", "w_in_a", "w_in_b", "w_in_r", "w_alpha", "b_alpha", "g_gla", "w_out", "g_cross", "w_cq", "w_co"):
        stacked, index = wts[name]
        args.append(stacked)
        specs.append(_const_spec(stacked, index))

    if kv_major:
        kv_shape = jax.ShapeDtypeStruct((n_prev + 1, bsz, SB_W, t), F32)
        kv_spec = pl.BlockSpec((n_prev + 1, 1, SB_W, tq), lambda b, i: (0, b, 0, i))
    else:
        kv_shape = jax.ShapeDtypeStruct((bsz, t, SB_W), F32)
        kv_spec = pl.BlockSpec((1, tq, SB_W), tile)
    out_shape = [jax.ShapeDtypeStruct((bsz, t, d), F32), kv_shape, kv_shape,
                 jax.ShapeDtypeStruct((bsz, GLA_HEADS, GLA_DK, GLA_DV), F32)]
    out_specs = [pl.BlockSpec((1, tq, d), tile), kv_spec, kv_spec,
                 pl.BlockSpec((1, GLA_HEADS, GLA_DK, GLA_DV), lambda b, i: (b, 0, 0, 0))]
    hist = past + pl.cdiv(t, SB_KEYS) * SB_KEYS
    scratch = [pltpu.VMEM((2, hist, SB_W), BF16), pltpu.VMEM((2, hist, SB_W), BF16),
               pltpu.VMEM((nmem, d), BF16), pltpu.VMEM((nmem, d), BF16),
               pltpu.VMEM((GLA_KW, GLA_DV), F32),
               pltpu.VMEM((tq, SB_W), BF16),
               pltpu.VMEM((SB_HEADS // 2, tq, LANES), F32),
               pltpu.VMEM((SB_HEADS, tq, LANES), F32)]
    return pl.pallas_call(
        functools.partial(_even_kernel, tq=tq, nt_static=t // tq, past=past, has_state=has_state,
                          n_prev=n_prev, kv_major=kv_major),
        grid=(bsz, t // tq),
        in_specs=specs, out_specs=out_specs, out_shape=out_shape, scratch_shapes=scratch,
        compiler_params=pltpu.CompilerParams(
            dimension_semantics=("arbitrary", "arbitrary"), vmem_limit_bytes=VMEM_LIMIT),
        name="even_layer",
    )(*args)


def _odd_kernel(*refs, tq, n_rows, cl, emit_vn, final):
    it = iter(refs)
    x_ref, mk_ref, mv_ref = next(it), next(it), next(it)
    gmix_ref, w_ref, gv_ref, bv_ref, wsp_ref, bsp_ref, wout_ref, gcr_ref, wq_ref, wo_ref = (
        next(it).at[0] for _ in range(10))
    gfin_ref = next(it).at[0] if final else None
    xo_ref = next(it)
    vn_ref = next(it) if emit_vn else None
    mkb, mvb = next(it), next(it)

    @pl.when(pl.program_id(1) == 0)
    def _init():
        mkb[...] = mk_ref[0].astype(BF16)
        mvb[...] = mv_ref[0].astype(BF16)

    d = x_ref.shape[-1]
    x = x_ref[...].reshape(n_rows * tq, d)
    h = _rms(x, gmix_ref[...]).astype(BF16)
    u = _dot(h, w_ref[:, 0:d])
    v = _dot(h, w_ref[:, d:2 * d])
    g = _dot(h, w_ref[:, 2 * d:3 * d])

    gv = _gelu(v)
    xc = gv - jnp.mean(gv, axis=-1, keepdims=True)
    v_n = xc * lax.rsqrt(jnp.mean(xc * xc, axis=-1, keepdims=True) + EPS) * gv_ref[...] + bv_ref[...]
    if emit_vn:
        vn_ref[...] = v_n.reshape(n_rows, tq, d)

    gw = d // SGU_GROUPS
    keep = _iota((cl, cl), 1) <= _iota((cl, cl), 0)
    vnb = v_n.astype(BF16)
    bsp = bsp_ref[0:cl, :]
    wgs = [jnp.where(keep, wsp_ref[gi, 0:cl, 0:cl], 0.0).astype(BF16) for gi in range(SGU_GROUPS)]
    rows = []
    for c in range(n_rows * tq // cl):
        cols = []
        for gi in range(SGU_GROUPS):
            cols.append(_dot(wgs[gi], vnb[c * cl:(c + 1) * cl, gi * gw:(gi + 1) * gw]) + bsp[:, gi:gi + 1])
        rows.append(jnp.concatenate(cols, axis=-1))
    s = jnp.concatenate(rows, axis=0) if len(rows) > 1 else rows[0]

    y = (_gelu(u) * s * _silu(g)).astype(BF16)
    x = x + _dot(y, wout_ref[...])
    x = _cross_attend(x, gcr_ref[...], wq_ref, wo_ref, [(mkb.at[r], mvb.at[r]) for r in range(n_rows)])
    if final:
        x = _rms(x, gfin_ref[...])
    xo_ref[...] = x.reshape(n_rows, tq, d)


def _odd_layer(x, mem_k, mem_v, layer, wts, tq, emit_vn, g_final):
    bsz, t, d = x.shape
    nmem = mem_k.shape[2]
    cl = min(SGU_CHUNK, t)
    final = g_final is not None
    assert t % tq == 0 and tq % cl == 0
    rows = ODD_ROWS if (t == tq and bsz % ODD_ROWS == 0 and ODD_ROWS * tq <= ODD_TILE) else 1

    tile = lambda b, i: (b, i, 0)
    args = [x, mem_k, mem_v]
    specs = [pl.BlockSpec((rows, tq, d), tile)]
    specs += [pl.BlockSpec((1, rows, nmem, d), lambda b, i: (layer, b, 0, 0))] * 2
    consts = [wts[name] for name in
              ("g_mix", "w_in", "g_sgu", "b_sgu", "w_sp", "b_sp_t", "w_out", "g_cross", "w_cq", "w_co")]
    if final:
        consts.append((g_final, 0))
    for stacked, index in consts:
        args.append(stacked)
        specs.append(_const_spec(stacked, index))

    out_shape = [jax.ShapeDtypeStruct((bsz, t, d), F32)]
    out_specs = [pl.BlockSpec((rows, tq, d), tile)]
    if emit_vn:
        out_shape.append(jax.ShapeDtypeStruct((bsz, t, d), F32))
        out_specs.append(pl.BlockSpec((rows, tq, d), tile))
    return pl.pallas_call(
        functools.partial(_odd_kernel, tq=tq, n_rows=rows, cl=cl, emit_vn=emit_vn, final=final),
        grid=(bsz // rows, t // tq),
        in_specs=specs, out_specs=out_specs, out_shape=out_shape,
        scratch_shapes=[pltpu.VMEM((rows, nmem, d), BF16), pltpu.VMEM((rows, nmem, d), BF16)],
        compiler_params=pltpu.CompilerParams(
            dimension_semantics=("arbitrary", "arbitrary"), vmem_limit_bytes=VMEM_LIMIT),
        name="odd_layer",
    )(*args)


EVEN_TILE = 256
MEM_ROWS = 2
ODD_ROWS = 4
ODD_TILE = 1024


def _tile_rows(t, layer):
    return min(EVEN_TILE if layer % 2 == 0 else ODD_TILE, t)


def _run_group(x, past_k, past_v, s0, mem_k, mem_v, even_w, odd_w, g_final, emit_vn):
    depth = mem_k.shape[0]
    bsz, t, d = x.shape
    tq = _tile_rows(t, 0)
    mem_k = mem_k.reshape(depth, bsz, -1, d)
    mem_v = mem_v.reshape(depth, bsz, -1, d)
    if past_k is not None:
        n_even, _, past = past_k.shape[:3]
        past_k = past_k.transpose(0, 1, 3, 4, 2).reshape(n_even, bsz, SB_W, past)
        past_v = past_v.transpose(0, 1, 3, 4, 2).reshape(n_even, bsz, SB_W, past)
    kv_major = tq % LANES == 0
    sb_k, sb_v, gla_s, sgu_v = [], [], [], []
    prev_kv = None
    for l in range(depth):
        if l % 2 == 0:
            x, kb, vb, s_new = _even_layer(x, past_k, past_v, s0, prev_kv, mem_k, mem_v, l, even_w[l], tq)
            if kv_major:
                prev_kv = (kb, vb)
            else:
                sb_k.append(kb.reshape(bsz, t, SB_HEADS, SB_DIM))
                sb_v.append(vb.reshape(bsz, t, SB_HEADS, SB_DIM))
            gla_s.append(s_new)
        else:
            outs = _odd_layer(x, mem_k, mem_v, l, odd_w[l], _tile_rows(t, l), emit_vn,
                              g_final if l == depth - 1 else None)
            x = outs[0]
            if emit_vn:
                sgu_v.append(outs[1])
    if kv_major:
        sb_k, sb_v = (a.reshape(a.shape[0], bsz, SB_HEADS, SB_DIM, t).transpose(0, 1, 4, 2, 3) for a in prev_kv)
    else:
        sb_k, sb_v = jnp.stack(sb_k), jnp.stack(sb_v)
    return x, sb_k, sb_v, jnp.stack(gla_s), (jnp.stack(sgu_v) if emit_vn else None)


def kernel(x_prompt, x_sample, cache_sb_k, cache_sb_v, state_gla, cache_mem_k, cache_mem_v, mem_prompt, g_mix, w_in_even, w_alpha, b_alpha, g_gla_out, w_out_even, w_in_odd, g_sgu_v, b_sgu_v, w_sp, b_sp, w_out_odd, g_cross, g_mem, w_cq, w_ck, w_cv, w_co, g_final):
    depth, d = g_mix.shape
    vec = lambda a: a[:, None, :]

    o_ra = C_QB
    w_in_r = jnp.concatenate(
        [w_in_even[:, :, o_ra:o_ra + GLA_RANK],
         jnp.zeros(w_in_even.shape[:2] + (RA_PAD - GLA_RANK,), w_in_even.dtype)], axis=2).astype(BF16)
    w_al = jnp.concatenate(
        [w_alpha, jnp.zeros((w_alpha.shape[0], RA_PAD - GLA_RANK, GLA_KW), w_alpha.dtype)], axis=1).astype(BF16)
    stacks = dict(
        g_mix=vec(g_mix), g_cross=vec(g_cross), w_cq=w_cq.astype(BF16), w_co=w_co.astype(BF16),
        w_in_a=w_in_even[:, :, :o_ra].astype(BF16), w_in_b=w_in_even[:, :, o_ra + GLA_RANK:].astype(BF16),
        w_in_r=w_in_r, w_alpha=w_al, b_alpha=vec(b_alpha), g_gla=vec(g_gla_out), w_out_e=w_out_even.astype(BF16),
        w_in_o=w_in_odd.astype(BF16), g_sgu=vec(g_sgu_v), b_sgu=vec(b_sgu_v), w_sp=w_sp,
        b_sp_t=b_sp.transpose(0, 2, 1), w_out_o=w_out_odd.astype(BF16))
    even_w, odd_w = {}, {}
    for l in range(depth):
        i = l // 2
        shared = {k: (stacks[k], l) for k in ("g_mix", "g_cross", "w_cq", "w_co")}
        if l % 2 == 0:
            even_w[l] = dict(shared, w_in_a=(stacks["w_in_a"], i), w_in_b=(stacks["w_in_b"], i),
                             w_in_r=(stacks["w_in_r"], i), w_alpha=(stacks["w_alpha"], i),
                             b_alpha=(stacks["b_alpha"], i), g_gla=(stacks["g_gla"], i),
                             w_out=(stacks["w_out_e"], i))
        else:
            odd_w[l] = dict(shared, w_in=(stacks["w_in_o"], i), g_sgu=(stacks["g_sgu"], i),
                            b_sgu=(stacks["b_sgu"], i), w_sp=(stacks["w_sp"], i),
                            b_sp_t=(stacks["b_sp_t"], i), w_out=(stacks["w_out_o"], i))

    mem_k_p, mem_v_p, mem_k_out, mem_v_out = _memkv(mem_prompt, g_mem, w_ck.astype(BF16), w_cv.astype(BF16))
    gfin = g_final.reshape(1, 1, -1)

    y_p, sbk_p, sbv_p, gla_p, _ = _run_group(
        x_prompt, None, None, None, mem_k_p, mem_v_p, even_w, odd_w, gfin, False)
    y_s, sbk_s, sbv_s, gla_s, sgu_s = _run_group(
        x_sample, cache_sb_k, cache_sb_v, state_gla, cache_mem_k, cache_mem_v, even_w, odd_w, gfin, True)

    return (y_p, y_s, sbk_p, sbv_p, gla_p, mem_k_out, mem_v_out, sbk_s, sbv_s, gla_s, sgu_s)
```

```python
import functools

import jax
import jax.numpy as jnp
from jax import lax
from jax.experimental import pallas as pl
from jax.experimental.pallas import tpu as pltpu

F32 = jnp.float32
BF16 = jnp.bfloat16

EPS = 1e-6
GLA_HEADS = 4
GLA_DK = 64
GLA_DV = 128
GLA_RANK = 16
GLA_TAU = 16.0
GLA_CHUNK = 64
GLA_KW = GLA_HEADS * GLA_DK
GLA_VW = GLA_HEADS * GLA_DV
SB_HEADS = 8
SB_DIM = 64
SB_W = SB_HEADS * SB_DIM
SB_KEYS = 128
SGU_GROUPS = 4
SGU_CHUNK = 128
X_HEADS = 4
LANES = 128
RA_PAD = LANES
VMEM_LIMIT = 56 * 1024 * 1024
LOG2_E = 1.4426950408889634
SB_MASKED_SCORE = -1e30
CARRY = object()
SB_UNDERFLOW_LOG2 = 160.0
GATE_SLABS = 2
GATE_GAP = 6
PAST_SLAB = 256

C_QA, C_KA, C_VA, C_GA = 0, 256, 512, 1024
C_QB, C_KB, C_VB, C_GB, C_RA, C_END = 1536, 2048, 2560, 3072, 3584, 3712


def _dot(a, b):
    return jnp.dot(a, b, preferred_element_type=F32)


def _dot_nt(a, b):
    return lax.dot_general(a, b, (((1,), (1,)), ((), ())), preferred_element_type=F32)


def _split2(a):
    hi = a.astype(BF16)
    lo = (a - hi.astype(F32)).astype(BF16)
    return hi, lo


def _rms(x, g):
    return x * lax.rsqrt(jnp.mean(x * x, axis=-1, keepdims=True) + EPS) * g


def _sigmoid(x):
    return 1.0 / (1.0 + jnp.exp(-x))


def _silu(x):
    return x * _sigmoid(x)


def _gelu(x):
    return x * (0.5 * (1.0 + jnp.tanh(0.7978845608028654 * (x + 0.044715 * (x * x * x)))))


def _softplus_neg_abs(z):
    return jnp.log1p(jnp.exp(-jnp.abs(z)))


def _iota(shape, dim):
    return lax.broadcasted_iota(jnp.int32, shape, dim)


def _cross_attend(x, g_cross, wq_ref, wo_ref, mems):
    d = x.shape[-1]
    hd = d // X_HEADS
    hc = _rms(x, g_cross).astype(BF16)
    q = (_dot(hc, wq_ref[...]) * (hd ** -0.5)).astype(BF16)
    n = x.shape[0] // len(mems)
    groups = []
    for r, (mkb_ref, mvb_ref) in enumerate(mems):
        outs = []
        for h in range(X_HEADS):
            sl = slice(h * hd, (h + 1) * hd)
            s = _dot_nt(q[r * n:(r + 1) * n, sl], mkb_ref[:, sl])
            e = jnp.exp(s - jnp.max(s, axis=-1, keepdims=True))
            l = jnp.sum(e, axis=-1, keepdims=True)
            outs.append(_dot(e.astype(BF16), mvb_ref[:, sl]) / l)
        groups.append(jnp.concatenate(outs, axis=-1))
    o = (jnp.concatenate(groups, axis=0) if len(groups) > 1 else groups[0]).astype(BF16)
    return x + _dot(o, wo_ref[...])


def _memkv_kernel(m_ref, g_ref, wk_ref, wv_ref, k_ref, v_ref, kh_ref, vh_ref):
    rows, n, d = m_ref.shape
    m = _rms(m_ref[...].reshape(rows * n, d), g_ref[0]).astype(BF16)
    k = _dot(m, wk_ref[0])
    v = _dot(m, wv_ref[0])
    hd = d // X_HEADS
    for r in range(rows):
        kr, vr = k[r * n:(r + 1) * n], v[r * n:(r + 1) * n]
        k_ref[0, r] = kr
        v_ref[0, r] = vr
        for h in range(X_HEADS):
            kh_ref[0, r, :, h, :] = kr[:, h * hd:(h + 1) * hd]
            vh_ref[0, r, :, h, :] = vr[:, h * hd:(h + 1) * hd]


def _memkv(mem, g_mem, w_ck, w_cv):
    b, n, d = mem.shape
    depth = g_mem.shape[0]
    hd = d // X_HEADS
    flat = jax.ShapeDtypeStruct((depth, b, n, d), F32)
    heads = jax.ShapeDtypeStruct((depth, b, n, X_HEADS, hd), F32)
    wspec = pl.BlockSpec((1, d, d), lambda l, i: (l, 0, 0))
    rows = MEM_ROWS if b % MEM_ROWS == 0 else 1
    fspec = pl.BlockSpec((1, rows, n, d), lambda l, i: (l, i, 0, 0))
    hspec = pl.BlockSpec((1, rows, n, X_HEADS, hd), lambda l, i: (l, i, 0, 0, 0))
    return pl.pallas_call(
        _memkv_kernel,
        grid=(depth, b // rows),
        in_specs=[pl.BlockSpec((rows, n, d), lambda l, i: (i, 0, 0)),
                  pl.BlockSpec((1, 1, d), lambda l, i: (l, 0, 0)),
                  wspec, wspec],
        out_specs=[fspec, fspec, hspec, hspec],
        out_shape=[flat, flat, heads, heads],
        compiler_params=pltpu.CompilerParams(
            dimension_semantics=("arbitrary", "arbitrary"), vmem_limit_bytes=VMEM_LIMIT),
        name="mem_kv",
    )(mem, g_mem.reshape(depth, 1, d), w_ck, w_cv)


def _even_kernel(*refs, tq, nt_static, past, has_state, n_prev, kv_major):
    it = iter(refs)
    x_ref = next(it)
    pk_ref = next(it) if past else None
    pv_ref = next(it) if past else None
    s0_ref = next(it) if has_state else None
    pko_ref = next(it) if n_prev else None
    pvo_ref = next(it) if n_prev else None
    mk_ref, mv_ref = next(it), next(it)
    gmix_ref, wa_ref, wb_ref, wr_ref, wal_ref, bal_ref, ggla_ref, wout_ref, gcr_ref, wq_ref, wo_ref = (
        next(it).at[0] for _ in range(11))
    xo_ref, ko_ref, vo_ref, so_ref = next(it), next(it), next(it), next(it)
    kscr, vscr, mkb, mvb, sst, qs, acc, car = (next(it) for _ in range(8))

    t = pl.program_id(1)
    nt = pl.num_programs(1)
    hist = kscr.shape[1]
    seq = tq * nt_static

    @pl.when(t == 0)
    def _init():
        mkb[...] = mk_ref[0, 0].astype(BF16)
        mvb[...] = mv_ref[0, 0].astype(BF16)
        half0 = (_iota((1, SB_W), 1) % LANES) < SB_DIM
        for c0 in range(0, past, PAST_SLAB):
            c1 = min(c0 + PAST_SLAB, past)
            pk = pk_ref[0, 0, :, c0:c1].T
            pv = pv_ref[0, 0, :, c0:c1].T
            for e in range(2):
                keep_e = half0 if e == 0 else jnp.logical_not(half0)
                kscr[e, c0:c1, :] = jnp.where(keep_e, pk, 0.0).astype(BF16)
                vscr[e, c0:c1, :] = jnp.where(keep_e, pv, 0.0).astype(BF16)
        for e in range(2):
            if hist > past + seq:
                kscr[e, past + seq:hist, :] = jnp.zeros((hist - past - seq, SB_W), BF16)
                vscr[e, past + seq:hist, :] = jnp.zeros((hist - past - seq, SB_W), BF16)
        if has_state:
            for h in range(GLA_HEADS):
                sst[h * GLA_DK:(h + 1) * GLA_DK, :] = s0_ref[0, 0, h]
        else:
            sst[...] = jnp.zeros(sst.shape, F32)

    x = x_ref[0]
    h = _rms(x, gmix_ref[...]).astype(BF16)

    def proj(c0, c1):
        for ref, lo, hi in ((wa_ref, 0, C_QB), (wb_ref, C_QB, C_RA), (wr_ref, C_RA, C_END)):
            if lo <= c0 and c1 <= hi:
                return _dot(h, ref[:, c0 - lo:c1 - lo])
        raise ValueError((c0, c1))

    qb = proj(C_QB, C_KB) * (SB_DIM ** -0.5 * LOG2_E)
    kb = proj(C_KB, C_VB)
    vb = proj(C_VB, C_GB)
    gla_out = []

    def gla_steps():
        ra = proj(C_RA, C_END).astype(BF16)
        yield
        qa = proj(C_QA, C_KA) * (GLA_DK ** -0.5)
        yield
        ka = proj(C_KA, C_VA)
        yield
        va = proj(C_VA, C_GA)
        yield
        za = _dot(ra, wal_ref[...]) + bal_ref[...]
        log_a = (jnp.minimum(za, 0.0) - _softplus_neg_abs(za)) * (1.0 / GLA_TAU)

        cl = GLA_CHUNK
        nc = tq // cl
        row_t = _iota((tq, tq), 0)
        col_t = _iota((tq, tq), 1)
        causal = ((row_t // cl) == (col_t // cl)) & (col_t <= row_t)
        ltri = jnp.where(causal, 1.0, 0.0).astype(BF16)
        g_hi, g_lo = _split2(log_a)
        b = _dot(jnp.concatenate([ltri, ltri], axis=1),
                 jnp.concatenate([g_hi, g_lo], axis=0))
        yield
        bt = b.T
        lane_t = _iota((1, tq), 1)
        b_last = [bt[:, (ci + 1) * cl - 1:(ci + 1) * cl] for ci in range(nc)]
        bl = jnp.broadcast_to(b_last[0], bt.shape)
        for ci in range(1, nc):
            bl = jnp.where(lane_t >= ci * cl, b_last[ci], bl)
        qeb = (qa * jnp.exp(b)).astype(BF16)
        ke = (ka * jnp.exp(-b)).astype(BF16)
        kdt = (ka.T * jnp.exp(bl - bt)).astype(BF16)
        vab = va.astype(BF16)
        yield
        lane_kw = _iota((1, GLA_KW), 1)
        zero_b = jnp.zeros((), BF16)
        qeh, o_intra, upd = [], [], []
        for hh in range(GLA_HEADS):
            ks = slice(hh * GLA_DK, (hh + 1) * GLA_DK)
            vs = slice(hh * GLA_DV, (hh + 1) * GLA_DV)
            q_h = jnp.where((lane_kw >= hh * GLA_DK) & (lane_kw < (hh + 1) * GLA_DK), qeb, zero_b)
            qeh.append(q_h)
            att = jnp.where(causal, _dot_nt(q_h, ke), 0.0).astype(BF16)
            o_intra.append(_dot(att, vab[:, vs]))
            yield
            kd_h = kdt[ks]
            kd_c = [jnp.where((lane_t >= ci * cl) & (lane_t < (ci + 1) * cl), kd_h, zero_b) for ci in range(nc)]
            upd.append(_dot(jnp.concatenate(kd_c, axis=0) if nc > 1 else kd_c[0], vab[:, vs]))
            yield
        s_all = sst[...]
        o_inter = []
        for ci in range(nc):
            rs = slice(ci * cl, (ci + 1) * cl)
            q_c = jnp.concatenate([qeh[hh][rs] for hh in range(GLA_HEADS)], axis=0)
            o_inter.append(_dot(q_c, s_all.astype(BF16)))
            u_c = jnp.concatenate([upd[hh][ci * GLA_DK:(ci + 1) * GLA_DK] for hh in range(GLA_HEADS)], axis=0)
            s_all = jnp.exp(b_last[ci]) * s_all + u_c
            yield
        sst[...] = s_all
        o_heads = []
        for hh in range(GLA_HEADS):
            inter = [o_inter[ci][hh * cl:(hh + 1) * cl] for ci in range(nc)]
            o_heads.append(o_intra[hh] + (jnp.concatenate(inter, axis=0) if nc > 1 else inter[0]))
        oa = jnp.concatenate(o_heads, axis=-1)
        ggla = ggla_ref[...]
        oa_n = []
        for hh in range(GLA_HEADS):
            vs = slice(hh * GLA_DV, (hh + 1) * GLA_DV)
            oa_n.append(_rms(oa[:, vs], ggla[:, vs]))
        gla_out.append(jnp.concatenate(oa_n, axis=-1))


    if kv_major:
        for j in range(n_prev):
            ko_ref[j, 0] = pko_ref[j, 0]
            vo_ref[j, 0] = pvo_ref[j, 0]
        ko_ref[n_prev, 0] = kb.T
        vo_ref[n_prev, 0] = vb.T
    else:
        ko_ref[0] = kb
        vo_ref[0] = vb
    half = (_iota((1, SB_W), 1) % LANES) < SB_DIM
    row0 = pl.multiple_of(past + t * tq, min(tq, SB_KEYS))
    kbb, vbb = kb.astype(BF16), vb.astype(BF16)
    zero_bf = jnp.zeros((), BF16)
    for e in range(2):
        keep_e = half if e == 0 else jnp.logical_not(half)
        kscr[e, pl.ds(row0, tq), :] = jnp.where(keep_e, kbb, zero_bf)
        vscr[e, pl.ds(row0, tq), :] = jnp.where(keep_e, vbb, zero_bf)
    qs[...] = qb.astype(BF16)
    acc[...] = jnp.zeros(acc.shape, F32)
    car[...] = jnp.zeros(car.shape, F32)

    ue_r = _iota((2 * SB_KEYS, 2 * LANES), 0) % SB_KEYS
    ue_c = _iota((2 * SB_KEYS, 2 * LANES), 1)
    ue2 = jnp.where((ue_c < LANES) | (ue_r >= ue_c - LANES), 1.0, 0.0).astype(BF16)
    n_pairs = SB_HEADS // 2

    def sb_block_steps(r0, rs, masked, gate=None):
        nr = tq - rs
        keys = pl.ds(r0, SB_KEYS)
        if masked:
            vis2 = (_iota((nr, 2 * SB_KEYS), 1) % SB_KEYS) < _iota((nr, 2 * SB_KEYS), 0)
        zs, lhs = [], []
        for p in range(n_pairs):
            ls_ = slice(p * LANES, (p + 1) * LANES)
            k2 = jnp.concatenate([kscr[0, keys, ls_], kscr[1, keys, ls_]], axis=0)
            z = _dot_nt(qs[rs:tq, ls_], k2)
            if masked:
                z = jnp.where(vis2, z, SB_MASKED_SCORE)
            sp = jnp.maximum(z, 0.0) + jnp.log2(1.0 + jnp.exp2(-jnp.abs(z)))
            zs.append(z)
            hi, lo = _split2(sp)
            for e in range(2):
                es = slice(e * SB_KEYS, (e + 1) * SB_KEYS)
                lhs.append(jnp.concatenate([hi[:, es], lo[:, es]], axis=1))
            yield
        sc = _dot(jnp.concatenate(lhs, axis=0), ue2)
        yield CARRY
        for p in range(n_pairs):
            ls_ = slice(p * LANES, (p + 1) * LANES)
            ws = []
            for e in range(2):
                hh = 2 * p + e
                sc_h = sc[hh * nr:(hh + 1) * nr]
                carry = car[hh, rs:tq, :]
                w = jnp.exp2(zs[p][:, e * SB_KEYS:(e + 1) * SB_KEYS] - sc_h[:, LANES:] - carry)
                ws.append(w.astype(BF16))
                car[hh, rs:tq, :] = carry + sc_h[:, :LANES]
            v2 = jnp.concatenate([vscr[0, keys, ls_], vscr[1, keys, ls_]], axis=0)
            if gate is not None:
                v2 = v2 * gate
            acc[p, rs:tq, :] += _dot(jnp.concatenate(ws, axis=1), v2)
            yield

    def pipelined(blocks):
        def head(g):
            for tok in g:
                if tok is CARRY:
                    return
                yield
        blocks = list(blocks)
        yield from head(blocks[0])
        for k, g in enumerate(blocks):
            if k + 1 < len(blocks):
                yield from head(blocks[k + 1])
            yield from g

    def interleave(streams):
        streams = list(streams)
        while streams:
            for g in list(streams):
                try:
                    next(g)
                except StopIteration:
                    streams.remove(g)

    nblk = (past + t * tq) // SB_KEYS
    step = 2 if (past % (2 * SB_KEYS) == 0 and tq % (2 * SB_KEYS) == 0) else 1

    def below_block(j, gate=None):
        return sb_block_steps(pl.multiple_of(j * SB_KEYS, SB_KEYS), 0, False, gate)

    gate = None if past >= step * SB_KEYS else (nblk >= step).astype(BF16)
    diag = [sb_block_steps(pl.multiple_of(row0 + dblk * SB_KEYS, SB_KEYS), dblk * SB_KEYS, True)
            for dblk in reversed(range(pl.cdiv(tq, SB_KEYS)))]
    first = [below_block(jnp.maximum(nblk - 1 - u, 0), gate) for u in range(step)]
    gates = []

    def gate_steps():
        for c0 in (C_GA, C_GB):
            for k in range(GATE_SLABS):
                for _ in range(GATE_GAP):
                    yield
                w = (C_QB - C_GA) // GATE_SLABS
                gates.append(proj(c0 + k * w, c0 + (k + 1) * w))
                yield

    interleave([pipelined(diag + first), gla_steps(), gate_steps()])
    ga = jnp.concatenate(gates[:GATE_SLABS], axis=-1)
    gb = jnp.concatenate(gates[GATE_SLABS:], axis=-1)
    oa = gla_out[0] * _silu(ga)

    @pl.when(t == nt - 1)
    def _emit_state():
        for hh in range(GLA_HEADS):
            so_ref[0, hh] = sst[hh * GLA_DK:(hh + 1) * GLA_DK, :]

    def still_live():
        lowest = car[0]
        for hh in range(1, SB_HEADS):
            lowest = jnp.minimum(lowest, car[hh])
        return (jnp.min(lowest) < SB_UNDERFLOW_LOG2).astype(jnp.int32)

    def more(c):
        return jnp.logical_and(c[0] < nblk, c[1] > 0)

    def below(c):
        i = c[0]
        interleave([pipelined([below_block(nblk - 1 - i - u) for u in range(step)])])
        return i + step, still_live()

    lax.while_loop(more, below, (jnp.int32(step), still_live()))

    ob = jnp.concatenate([acc[p] for p in range(n_pairs)], axis=-1)
    ob = ob * _silu(gb)

    mix = jnp.concatenate([oa, ob], axis=-1).astype(BF16)
    x = x + _dot(mix, wout_ref[...])
    xo_ref[0] = _cross_attend(x, gcr_ref[...], wq_ref, wo_ref, [(mkb, mvb)])


def _const_spec(stacked, index):
    tail = stacked.shape[1:]
    return pl.BlockSpec((1,) + tail, lambda *_: (index,) + (0,) * len(tail), pipeline_mode=pl.Buffered(1))


def _even_layer(x, past_k, past_v, s0, prev_kv, mem_k, mem_v, layer, wts, tq):
    bsz, t, d = x.shape
    past = 0 if past_k is None else past_k.shape[-1]
    has_state = s0 is not None
    nmem = mem_k.shape[2]
    li = layer // 2
    kv_major = tq % LANES == 0
    n_prev = 0 if prev_kv is None else prev_kv[0].shape[0]
    assert kv_major or prev_kv is None
    assert t % tq == 0 and tq % GLA_CHUNK == 0 and past % SB_KEYS == 0
    assert (tq % SB_KEYS == 0 or t == tq) and SB_KEYS == LANES

    tile = lambda b, i: (b, i, 0)
    args, specs = [x], [pl.BlockSpec((1, tq, d), tile)]
    if past:
        args += [past_k, past_v]
        specs += [pl.BlockSpec((1, 1, SB_W, past), lambda b, i: (li, b, 0, 0))] * 2
    if has_state:
        args.append(s0)
        specs.append(pl.BlockSpec((1, 1, GLA_HEADS, GLA_DK, GLA_DV), lambda b, i: (li, b, 0, 0, 0)))
    if n_prev:
        args += list(prev_kv)
        specs += [pl.BlockSpec((n_prev, 1, SB_W, tq), lambda b, i: (0, b, 0, i))] * 2
    args += [mem_k, mem_v]
    specs += [pl.BlockSpec((1, 1, nmem, d), lambda b, i: (layer, b, 0, 0))] * 2
    for name in ("g_mix", "w_in_a", "w_in_b", "w_in_r", "w_alpha", "b_alpha", "g_gla", "w_out", "g_cross", "w_cq", "w_co"):
        stacked, index = wts[name]
        args.append(stacked)
        specs.append(_const_spec(stacked, index))

    if kv_major:
        kv_shape = jax.ShapeDtypeStruct((n_prev + 1, bsz, SB_W, t), F32)
        kv_spec = pl.BlockSpec((n_prev + 1, 1, SB_W, tq), lambda b, i: (0, b, 0, i))
    else:
        kv_shape = jax.ShapeDtypeStruct((bsz, t, SB_W), F32)
        kv_spec = pl.BlockSpec((1, tq, SB_W), tile)
    out_shape = [jax.ShapeDtypeStruct((bsz, t, d), F32), kv_shape, kv_shape,
                 jax.ShapeDtypeStruct((bsz, GLA_HEADS, GLA_DK, GLA_DV), F32)]
    out_specs = [pl.BlockSpec((1, tq, d), tile), kv_spec, kv_spec,
                 pl.BlockSpec((1, GLA_HEADS, GLA_DK, GLA_DV), lambda b, i: (b, 0, 0, 0))]
    hist = past + pl.cdiv(t, SB_KEYS) * SB_KEYS
    scratch = [pltpu.VMEM((2, hist, SB_W), BF16), pltpu.VMEM((2, hist, SB_W), BF16),
               pltpu.VMEM((nmem, d), BF16), pltpu.VMEM((nmem, d), BF16),
               pltpu.VMEM((GLA_KW, GLA_DV), F32),
               pltpu.VMEM((tq, SB_W), BF16),
               pltpu.VMEM((SB_HEADS // 2, tq, LANES), F32),
               pltpu.VMEM((SB_HEADS, tq, LANES), F32)]
    return pl.pallas_call(
        functools.partial(_even_kernel, tq=tq, nt_static=t // tq, past=past, has_state=has_state,
                          n_prev=n_prev, kv_major=kv_major),
        grid=(bsz, t // tq),
        in_specs=specs, out_specs=out_specs, out_shape=out_shape, scratch_shapes=scratch,
        compiler_params=pltpu.CompilerParams(
            dimension_semantics=("arbitrary", "arbitrary"), vmem_limit_bytes=VMEM_LIMIT),
        name="even_layer",
    )(*args)


def _odd_kernel(*refs, tq, n_rows, cl, emit_vn, final):
    it = iter(refs)
    x_ref, mk_ref, mv_ref = next(it), next(it), next(it)
    gmix_ref, w_ref, gv_ref, bv_ref, wsp_ref, bsp_ref, wout_ref, gcr_ref, wq_ref, wo_ref = (
        next(it).at[0] for _ in range(10))
    gfin_ref = next(it).at[0] if final else None
    xo_ref = next(it)
    vn_ref = next(it) if emit_vn else None
    mkb, mvb = next(it), next(it)

    @pl.when(pl.program_id(1) == 0)
    def _init():
        mkb[...] = mk_ref[0].astype(BF16)
        mvb[...] = mv_ref[0].astype(BF16)

    d = x_ref.shape[-1]
    x = x_ref[...].reshape(n_rows * tq, d)
    h = _rms(x, gmix_ref[...]).astype(BF16)
    u = _dot(h, w_ref[:, 0:d])
    v = _dot(h, w_ref[:, d:2 * d])
    g = _dot(h, w_ref[:, 2 * d:3 * d])

    gv = _gelu(v)
    xc = gv - jnp.mean(gv, axis=-1, keepdims=True)
    v_n = xc * lax.rsqrt(jnp.mean(xc * xc, axis=-1, keepdims=True) + EPS) * gv_ref[...] + bv_ref[...]
    if emit_vn:
        vn_ref[...] = v_n.reshape(n_rows, tq, d)

    gw = d // SGU_GROUPS
    keep = _iota((cl, cl), 1) <= _iota((cl, cl), 0)
    vnb = v_n.astype(BF16)
    bsp = bsp_ref[0:cl, :]
    wgs = [jnp.where(keep, wsp_ref[gi, 0:cl, 0:cl], 0.0).astype(BF16) for gi in range(SGU_GROUPS)]
    rows = []
    for c in range(n_rows * tq // cl):
        cols = []
        for gi in range(SGU_GROUPS):
            cols.append(_dot(wgs[gi], vnb[c * cl:(c + 1) * cl, gi * gw:(gi + 1) * gw]) + bsp[:, gi:gi + 1])
        rows.append(jnp.concatenate(cols, axis=-1))
    s = jnp.concatenate(rows, axis=0) if len(rows) > 1 else rows[0]

    y = (_gelu(u) * s * _silu(g)).astype(BF16)
    x = x + _dot(y, wout_ref[...])
    x = _cross_attend(x, gcr_ref[...], wq_ref, wo_ref, [(mkb.at[r], mvb.at[r]) for r in range(n_rows)])
    if final:
        x = _rms(x, gfin_ref[...])
    xo_ref[...] = x.reshape(n_rows, tq, d)


def _odd_layer(x, mem_k, mem_v, layer, wts, tq, emit_vn, g_final):
    bsz, t, d = x.shape
    nmem = mem_k.shape[2]
    cl = min(SGU_CHUNK, t)
    final = g_final is not None
    assert t % tq == 0 and tq % cl == 0
    rows = ODD_ROWS if (t == tq and bsz % ODD_ROWS == 0 and ODD_ROWS * tq <= ODD_TILE) else 1

    tile = lambda b, i: (b, i, 0)
    args = [x, mem_k, mem_v]
    specs = [pl.BlockSpec((rows, tq, d), tile)]
    specs += [pl.BlockSpec((1, rows, nmem, d), lambda b, i: (layer, b, 0, 0))] * 2
    consts = [wts[name] for name in
              ("g_mix", "w_in", "g_sgu", "b_sgu", "w_sp", "b_sp_t", "w_out", "g_cross", "w_cq", "w_co")]
    if final:
        consts.append((g_final, 0))
    for stacked, index in consts:
        args.append(stacked)
        specs.append(_const_spec(stacked, index))

    out_shape = [jax.ShapeDtypeStruct((bsz, t, d), F32)]
    out_specs = [pl.BlockSpec((rows, tq, d), tile)]
    if emit_vn:
        out_shape.append(jax.ShapeDtypeStruct((bsz, t, d), F32))
        out_specs.append(pl.BlockSpec((rows, tq, d), tile))
    return pl.pallas_call(
        functools.partial(_odd_kernel, tq=tq, n_rows=rows, cl=cl, emit_vn=emit_vn, final=final),
        grid=(bsz // rows, t // tq),
        in_specs=specs, out_specs=out_specs, out_shape=out_shape,
        scratch_shapes=[pltpu.VMEM((rows, nmem, d), BF16), pltpu.VMEM((rows, nmem, d), BF16)],
        compiler_params=pltpu.CompilerParams(
            dimension_semantics=("arbitrary", "arbitrary"), vmem_limit_bytes=VMEM_LIMIT),
        name="odd_layer",
    )(*args)


EVEN_TILE = 256
MEM_ROWS = 4
ODD_ROWS = 4
ODD_TILE = 1024


def _tile_rows(t, layer):
    return min(EVEN_TILE if layer % 2 == 0 else ODD_TILE, t)


def _run_group(x, past_k, past_v, s0, mem_k, mem_v, even_w, odd_w, g_final, emit_vn):
    depth = mem_k.shape[0]
    bsz, t, d = x.shape
    tq = _tile_rows(t, 0)
    mem_k = mem_k.reshape(depth, bsz, -1, d)
    mem_v = mem_v.reshape(depth, bsz, -1, d)
    if past_k is not None:
        n_even, _, past = past_k.shape[:3]
        past_k = past_k.transpose(0, 1, 3, 4, 2).reshape(n_even, bsz, SB_W, past)
        past_v = past_v.transpose(0, 1, 3, 4, 2).reshape(n_even, bsz, SB_W, past)
    kv_major = tq % LANES == 0
    sb_k, sb_v, gla_s, sgu_v = [], [], [], []
    prev_kv = None
    for l in range(depth):
        if l % 2 == 0:
            x, kb, vb, s_new = _even_layer(x, past_k, past_v, s0, prev_kv, mem_k, mem_v, l, even_w[l], tq)
            if kv_major:
                prev_kv = (kb, vb)
            else:
                sb_k.append(kb.reshape(bsz, t, SB_HEADS, SB_DIM))
                sb_v.append(vb.reshape(bsz, t, SB_HEADS, SB_DIM))
            gla_s.append(s_new)
        else:
            outs = _odd_layer(x, mem_k, mem_v, l, odd_w[l], _tile_rows(t, l), emit_vn,
                              g_final if l == depth - 1 else None)
            x = outs[0]
            if emit_vn:
                sgu_v.append(outs[1])
    if kv_major:
        sb_k, sb_v = (a.reshape(a.shape[0], bsz, SB_HEADS, SB_DIM, t).transpose(0, 1, 4, 2, 3) for a in prev_kv)
    else:
        sb_k, sb_v = jnp.stack(sb_k), jnp.stack(sb_v)
    return x, sb_k, sb_v, jnp.stack(gla_s), (jnp.stack(sgu_v) if emit_vn else None)


def kernel(x_prompt, x_sample, cache_sb_k, cache_sb_v, state_gla, cache_mem_k, cache_mem_v, mem_prompt, g_mix, w_in_even, w_alpha, b_alpha, g_gla_out, w_out_even, w_in_odd, g_sgu_v, b_sgu_v, w_sp, b_sp, w_out_odd, g_cross, g_mem, w_cq, w_ck, w_cv, w_co, g_final):
    depth, d = g_mix.shape
    vec = lambda a: a[:, None, :]

    o_ra = C_QB
    w_in_r = jnp.concatenate(
        [w_in_even[:, :, o_ra:o_ra + GLA_RANK],
         jnp.zeros(w_in_even.shape[:2] + (RA_PAD - GLA_RANK,), w_in_even.dtype)], axis=2).astype(BF16)
    w_al = jnp.concatenate(
        [w_alpha, jnp.zeros((w_alpha.shape[0], RA_PAD - GLA_RANK, GLA_KW), w_alpha.dtype)], axis=1).astype(BF16)
    stacks = dict(
        g_mix=vec(g_mix), g_cross=vec(g_cross), w_cq=w_cq.astype(BF16), w_co=w_co.astype(BF16),
        w_in_a=w_in_even[:, :, :o_ra].astype(BF16), w_in_b=w_in_even[:, :, o_ra + GLA_RANK:].astype(BF16),
        w_in_r=w_in_r, w_alpha=w_al, b_alpha=vec(b_alpha), g_gla=vec(g_gla_out), w_out_e=w_out_even.astype(BF16),
        w_in_o=w_in_odd.astype(BF16), g_sgu=vec(g_sgu_v), b_sgu=vec(b_sgu_v), w_sp=w_sp,
        b_sp_t=b_sp.transpose(0, 2, 1), w_out_o=w_out_odd.astype(BF16))
    even_w, odd_w = {}, {}
    for l in range(depth):
        i = l // 2
        shared = {k: (stacks[k], l) for k in ("g_mix", "g_cross", "w_cq", "w_co")}
        if l % 2 == 0:
            even_w[l] = dict(shared, w_in_a=(stacks["w_in_a"], i), w_in_b=(stacks["w_in_b"], i),
                             w_in_r=(stacks["w_in_r"], i), w_alpha=(stacks["w_alpha"], i),
                             b_alpha=(stacks["b_alpha"], i), g_gla=(stacks["g_gla"], i),
                             w_out=(stacks["w_out_e"], i))
        else:
            odd_w[l] = dict(shared, w_in=(stacks["w_in_o"], i), g_sgu=(stacks["g_sgu"], i),
                            b_sgu=(stacks["b_sgu"], i), w_sp=(stacks["w_sp"], i),
                            b_sp_t=(stacks["b_sp_t"], i), w_out=(stacks["w_out_o"], i))

    mem_k_p, mem_v_p, mem_k_out, mem_v_out = _memkv(mem_prompt, g_mem, w_ck.astype(BF16), w_cv.astype(BF16))
    gfin = g_final.reshape(1, 1, -1)

    y_p, sbk_p, sbv_p, gla_p, _ = _run_group(
        x_prompt, None, None, None, mem_k_p, mem_v_p, even_w, odd_w, gfin, False)
    y_s, sbk_s, sbv_s, gla_s, sgu_s = _run_group(
        x_sample, cache_sb_k, cache_sb_v, state_gla, cache_mem_k, cache_mem_v, even_w, odd_w, gfin, True)

    return (y_p, y_s, sbk_p, sbv_p, gla_p, mem_k_out, mem_v_out, sbk_s, sbv_s, gla_s, sgu_s)
```

```python
import functools

import jax
import jax.numpy as jnp
from jax import lax
from jax.experimental import pallas as pl
from jax.experimental.pallas import tpu as pltpu

F32 = jnp.float32
BF16 = jnp.bfloat16

EPS = 1e-6
GLA_HEADS = 4
GLA_DK = 64
GLA_DV = 128
GLA_RANK = 16
GLA_TAU = 16.0
GLA_CHUNK = 64
GLA_KW = GLA_HEADS * GLA_DK
GLA_VW = GLA_HEADS * GLA_DV
SB_HEADS = 8
SB_DIM = 64
SB_W = SB_HEADS * SB_DIM
SB_KEYS = 128
SGU_GROUPS = 4
SGU_CHUNK = 128
X_HEADS = 4
LANES = 128
RA_PAD = LANES
VMEM_LIMIT = 56 * 1024 * 1024
LOG2_E = 1.4426950408889634
SB_MASKED_SCORE = -1e30
CARRY = object()
SB_UNDERFLOW_LOG2 = 160.0
GATE_SLABS = 2
GATE_GAP = 6
PAST_SLAB = 256

C_QA, C_KA, C_VA, C_GA = 0, 256, 512, 1024
C_QB, C_KB, C_VB, C_GB, C_RA, C_END = 1536, 2048, 2560, 3072, 3584, 3712


def _dot(a, b):
    return jnp.dot(a, b, preferred_element_type=F32)


def _dot_nt(a, b):
    return lax.dot_general(a, b, (((1,), (1,)), ((), ())), preferred_element_type=F32)


def _split2(a):
    hi = a.astype(BF16)
    lo = (a - hi.astype(F32)).astype(BF16)
    return hi, lo


def _rms(x, g):
    return x * lax.rsqrt(jnp.mean(x * x, axis=-1, keepdims=True) + EPS) * g


def _sigmoid(x):
    return 1.0 / (1.0 + jnp.exp(-x))


def _silu(x):
    return x * _sigmoid(x)


def _gelu(x):
    return x * (0.5 * (1.0 + jnp.tanh(0.7978845608028654 * (x + 0.044715 * (x * x * x)))))


def _softplus_neg_abs(z):
    return jnp.log1p(jnp.exp(-jnp.abs(z)))


def _iota(shape, dim):
    return lax.broadcasted_iota(jnp.int32, shape, dim)


def _cross_attend(x, g_cross, wq_ref, wo_ref, mems):
    d = x.shape[-1]
    hd = d // X_HEADS
    hc = _rms(x, g_cross).astype(BF16)
    q = (_dot(hc, wq_ref[...]) * (hd ** -0.5)).astype(BF16)
    n = x.shape[0] // len(mems)
    groups = []
    for r, (mkb_ref, mvb_ref) in enumerate(mems):
        outs = []
        for h in range(X_HEADS):
            sl = slice(h * hd, (h + 1) * hd)
            s = _dot_nt(q[r * n:(r + 1) * n, sl], mkb_ref[:, sl])
            e = jnp.exp(s - jnp.max(s, axis=-1, keepdims=True))
            l = jnp.sum(e, axis=-1, keepdims=True)
            outs.append(_dot(e.astype(BF16), mvb_ref[:, sl]) / l)
        groups.append(jnp.concatenate(outs, axis=-1))
    o = (jnp.concatenate(groups, axis=0) if len(groups) > 1 else groups[0]).astype(BF16)
    return x + _dot(o, wo_ref[...])


def _memkv_kernel(m_ref, g_ref, wk_ref, wv_ref, k_ref, v_ref, kh_ref, vh_ref):
    rows, n, d = m_ref.shape
    m = _rms(m_ref[...].reshape(rows * n, d), g_ref[0]).astype(BF16)
    k = _dot(m, wk_ref[0])
    v = _dot(m, wv_ref[0])
    hd = d // X_HEADS
    for r in range(rows):
        kr, vr = k[r * n:(r + 1) * n], v[r * n:(r + 1) * n]
        k_ref[0, r] = kr
        v_ref[0, r] = vr
        for h in range(X_HEADS):
            kh_ref[0, r, :, h, :] = kr[:, h * hd:(h + 1) * hd]
            vh_ref[0, r, :, h, :] = vr[:, h * hd:(h + 1) * hd]


def _memkv(mem, g_mem, w_ck, w_cv):
    b, n, d = mem.shape
    depth = g_mem.shape[0]
    hd = d // X_HEADS
    flat = jax.ShapeDtypeStruct((depth, b, n, d), F32)
    heads = jax.ShapeDtypeStruct((depth, b, n, X_HEADS, hd), F32)
    wspec = pl.BlockSpec((1, d, d), lambda l, i: (l, 0, 0))
    rows = MEM_ROWS if b % MEM_ROWS == 0 else 1
    fspec = pl.BlockSpec((1, rows, n, d), lambda l, i: (l, i, 0, 0))
    hspec = pl.BlockSpec((1, rows, n, X_HEADS, hd), lambda l, i: (l, i, 0, 0, 0))
    return pl.pallas_call(
        _memkv_kernel,
        grid=(depth, b // rows),
        in_specs=[pl.BlockSpec((rows, n, d), lambda l, i: (i, 0, 0)),
                  pl.BlockSpec((1, 1, d), lambda l, i: (l, 0, 0)),
                  wspec, wspec],
        out_specs=[fspec, fspec, hspec, hspec],
        out_shape=[flat, flat, heads, heads],
        compiler_params=pltpu.CompilerParams(
            dimension_semantics=("arbitrary", "arbitrary"), vmem_limit_bytes=VMEM_LIMIT),
        name="mem_kv",
    )(mem, g_mem.reshape(depth, 1, d), w_ck, w_cv)


def _even_kernel(*refs, tq, nt_static, past, has_state, n_prev, kv_major):
    it = iter(refs)
    x_ref = next(it)
    pk_ref = next(it) if past else None
    pv_ref = next(it) if past else None
    s0_ref = next(it) if has_state else None
    pko_ref = next(it) if n_prev else None
    pvo_ref = next(it) if n_prev else None
    mk_ref, mv_ref = next(it), next(it)
    gmix_ref, wa_ref, wb_ref, wr_ref, wal_ref, bal_ref, ggla_ref, wout_ref, gcr_ref, wq_ref, wo_ref = (
        next(it).at[0] for _ in range(11))
    xo_ref, ko_ref, vo_ref, so_ref = next(it), next(it), next(it), next(it)
    kscr, vscr, mkb, mvb, sst, qs, acc, car = (next(it) for _ in range(8))

    t = pl.program_id(1)
    nt = pl.num_programs(1)
    hist = kscr.shape[1]
    seq = tq * nt_static

    @pl.when(t == 0)
    def _init():
        mkb[...] = mk_ref[0, 0].astype(BF16)
        mvb[...] = mv_ref[0, 0].astype(BF16)
        half0 = (_iota((1, SB_W), 1) % LANES) < SB_DIM
        for c0 in range(0, past, PAST_SLAB):
            c1 = min(c0 + PAST_SLAB, past)
            pk = pk_ref[0, 0, :, c0:c1].T
            pv = pv_ref[0, 0, :, c0:c1].T
            for e in range(2):
                keep_e = half0 if e == 0 else jnp.logical_not(half0)
                kscr[e, c0:c1, :] = jnp.where(keep_e, pk, 0.0).astype(BF16)
                vscr[e, c0:c1, :] = jnp.where(keep_e, pv, 0.0).astype(BF16)
        for e in range(2):
            if hist > past + seq:
                kscr[e, past + seq:hist, :] = jnp.zeros((hist - past - seq, SB_W), BF16)
                vscr[e, past + seq:hist, :] = jnp.zeros((hist - past - seq, SB_W), BF16)
        if has_state:
            for h in range(GLA_HEADS):
                sst[h * GLA_DK:(h + 1) * GLA_DK, :] = s0_ref[0, 0, h]
        else:
            sst[...] = jnp.zeros(sst.shape, F32)

    x = x_ref[0]
    h = _rms(x, gmix_ref[...]).astype(BF16)

    def proj(c0, c1):
        for ref, lo, hi in ((wa_ref, 0, C_QB), (wb_ref, C_QB, C_RA), (wr_ref, C_RA, C_END)):
            if lo <= c0 and c1 <= hi:
                return _dot(h, ref[:, c0 - lo:c1 - lo])
        raise ValueError((c0, c1))

    qb = proj(C_QB, C_KB) * (SB_DIM ** -0.5 * LOG2_E)
    kb = proj(C_KB, C_VB)
    vb = proj(C_VB, C_GB)
    gla_out = []

    def gla_steps():
        ra = proj(C_RA, C_END).astype(BF16)
        yield
        qa = proj(C_QA, C_KA) * (GLA_DK ** -0.5)
        yield
        ka = proj(C_KA, C_VA)
        yield
        va = proj(C_VA, C_GA)
        yield
        za = _dot(ra, wal_ref[...]) + bal_ref[...]
        log_a = (jnp.minimum(za, 0.0) - _softplus_neg_abs(za)) * (1.0 / GLA_TAU)

        cl = GLA_CHUNK
        nc = tq // cl
        row_t = _iota((tq, tq), 0)
        col_t = _iota((tq, tq), 1)
        causal = ((row_t // cl) == (col_t // cl)) & (col_t <= row_t)
        ltri = jnp.where(causal, 1.0, 0.0).astype(BF16)
        g_hi, g_lo = _split2(log_a)
        b = _dot(jnp.concatenate([ltri, ltri], axis=1),
                 jnp.concatenate([g_hi, g_lo], axis=0))
        yield
        bt = b.T
        lane_t = _iota((1, tq), 1)
        b_last = [bt[:, (ci + 1) * cl - 1:(ci + 1) * cl] for ci in range(nc)]
        bl = jnp.broadcast_to(b_last[0], bt.shape)
        for ci in range(1, nc):
            bl = jnp.where(lane_t >= ci * cl, b_last[ci], bl)
        qeb = (qa * jnp.exp(b)).astype(BF16)
        ke = (ka * jnp.exp(-b)).astype(BF16)
        kdt = (ka.T * jnp.exp(bl - bt)).astype(BF16)
        vab = va.astype(BF16)
        yield
        lane_kw = _iota((1, GLA_KW), 1)
        zero_b = jnp.zeros((), BF16)
        qeh, o_intra, upd = [], [], []
        for hh in range(GLA_HEADS):
            ks = slice(hh * GLA_DK, (hh + 1) * GLA_DK)
            vs = slice(hh * GLA_DV, (hh + 1) * GLA_DV)
            q_h = jnp.where((lane_kw >= hh * GLA_DK) & (lane_kw < (hh + 1) * GLA_DK), qeb, zero_b)
            qeh.append(q_h)
            att = jnp.where(causal, _dot_nt(q_h, ke), 0.0).astype(BF16)
            o_intra.append(_dot(att, vab[:, vs]))
            yield
            kd_h = kdt[ks]
            kd_c = [jnp.where((lane_t >= ci * cl) & (lane_t < (ci + 1) * cl), kd_h, zero_b) for ci in range(nc)]
            upd.append(_dot(jnp.concatenate(kd_c, axis=0) if nc > 1 else kd_c[0], vab[:, vs]))
            yield
        s_all = sst[...]
        o_inter = []
        for ci in range(nc):
            rs = slice(ci * cl, (ci + 1) * cl)
            q_c = jnp.concatenate([qeh[hh][rs] for hh in range(GLA_HEADS)], axis=0)
            o_inter.append(_dot(q_c, s_all.astype(BF16)))
            u_c = jnp.concatenate([upd[hh][ci * GLA_DK:(ci + 1) * GLA_DK] for hh in range(GLA_HEADS)], axis=0)
            s_all = jnp.exp(b_last[ci]) * s_all + u_c
            yield
        sst[...] = s_all
        o_heads = []
        for hh in range(GLA_HEADS):
            inter = [o_inter[ci][hh * cl:(hh + 1) * cl] for ci in range(nc)]
            o_heads.append(o_intra[hh] + (jnp.concatenate(inter, axis=0) if nc > 1 else inter[0]))
        oa = jnp.concatenate(o_heads, axis=-1)
        ggla = ggla_ref[...]
        oa_n = []
        for hh in range(GLA_HEADS):
            vs = slice(hh * GLA_DV, (hh + 1) * GLA_DV)
            oa_n.append(_rms(oa[:, vs], ggla[:, vs]))
        gla_out.append(jnp.concatenate(oa_n, axis=-1))


    if kv_major:
        for j in range(n_prev):
            ko_ref[j, 0] = pko_ref[j, 0]
            vo_ref[j, 0] = pvo_ref[j, 0]
        ko_ref[n_prev, 0] = kb.T
        vo_ref[n_prev, 0] = vb.T
    else:
        ko_ref[0] = kb
        vo_ref[0] = vb
    half = (_iota((1, SB_W), 1) % LANES) < SB_DIM
    row0 = pl.multiple_of(past + t * tq, min(tq, SB_KEYS))
    kbb, vbb = kb.astype(BF16), vb.astype(BF16)
    zero_bf = jnp.zeros((), BF16)
    for e in range(2):
        keep_e = half if e == 0 else jnp.logical_not(half)
        kscr[e, pl.ds(row0, tq), :] = jnp.where(keep_e, kbb, zero_bf)
        vscr[e, pl.ds(row0, tq), :] = jnp.where(keep_e, vbb, zero_bf)
    qs[...] = qb.astype(BF16)
    acc[...] = jnp.zeros(acc.shape, F32)
    car[...] = jnp.zeros(car.shape, F32)

    ue_r = _iota((2 * SB_KEYS, 2 * LANES), 0) % SB_KEYS
    ue_c = _iota((2 * SB_KEYS, 2 * LANES), 1)
    ue2 = jnp.where((ue_c < LANES) | (ue_r >= ue_c - LANES), 1.0, 0.0).astype(BF16)
    n_pairs = SB_HEADS // 2

    def sb_block_steps(r0, rs, masked, gate=None):
        nr = tq - rs
        keys = pl.ds(r0, SB_KEYS)
        if masked:
            vis2 = (_iota((nr, 2 * SB_KEYS), 1) % SB_KEYS) < _iota((nr, 2 * SB_KEYS), 0)
        zs, lhs = [], []
        for p in range(n_pairs):
            ls_ = slice(p * LANES, (p + 1) * LANES)
            k2 = jnp.concatenate([kscr[0, keys, ls_], kscr[1, keys, ls_]], axis=0)
            z = _dot_nt(qs[rs:tq, ls_], k2)
            if masked:
                z = jnp.where(vis2, z, SB_MASKED_SCORE)
            sp = jnp.maximum(z, 0.0) + jnp.log2(1.0 + jnp.exp2(-jnp.abs(z)))
            zs.append(z)
            hi, lo = _split2(sp)
            for e in range(2):
                es = slice(e * SB_KEYS, (e + 1) * SB_KEYS)
                lhs.append(jnp.concatenate([hi[:, es], lo[:, es]], axis=1))
            yield
        sc = _dot(jnp.concatenate(lhs, axis=0), ue2)
        yield CARRY
        for p in range(n_pairs):
            ls_ = slice(p * LANES, (p + 1) * LANES)
            ws = []
            for e in range(2):
                hh = 2 * p + e
                sc_h = sc[hh * nr:(hh + 1) * nr]
                carry = car[hh, rs:tq, :]
                w = jnp.exp2(zs[p][:, e * SB_KEYS:(e + 1) * SB_KEYS] - sc_h[:, LANES:] - carry)
                ws.append(w.astype(BF16))
                car[hh, rs:tq, :] = carry + sc_h[:, :LANES]
            v2 = jnp.concatenate([vscr[0, keys, ls_], vscr[1, keys, ls_]], axis=0)
            if gate is not None:
                v2 = v2 * gate
            acc[p, rs:tq, :] += _dot(jnp.concatenate(ws, axis=1), v2)
            yield

    def pipelined(blocks):
        def head(g):
            for tok in g:
                if tok is CARRY:
                    return
                yield
        blocks = list(blocks)
        yield from head(blocks[0])
        for k, g in enumerate(blocks):
            if k + 1 < len(blocks):
                yield from head(blocks[k + 1])
            yield from g

    def interleave(streams):
        streams = list(streams)
        while streams:
            for g in list(streams):
                try:
                    next(g)
                except StopIteration:
                    streams.remove(g)

    nblk = (past + t * tq) // SB_KEYS
    step = 2 if (past % (2 * SB_KEYS) == 0 and (tq % (2 * SB_KEYS) == 0 or nt_static == 1)) else 1

    def below_block(j, gate=None):
        return sb_block_steps(pl.multiple_of(j * SB_KEYS, SB_KEYS), 0, False, gate)

    gate = None if past >= step * SB_KEYS else (nblk >= step).astype(BF16)
    diag = [sb_block_steps(pl.multiple_of(row0 + dblk * SB_KEYS, SB_KEYS), dblk * SB_KEYS, True)
            for dblk in reversed(range(pl.cdiv(tq, SB_KEYS)))]
    first = [below_block(jnp.maximum(nblk - 1 - u, 0), gate) for u in range(step)]
    gates = []

    def gate_steps():
        for c0 in (C_GA, C_GB):
            for k in range(GATE_SLABS):
                for _ in range(GATE_GAP):
                    yield
                w = (C_QB - C_GA) // GATE_SLABS
                gates.append(proj(c0 + k * w, c0 + (k + 1) * w))
                yield

    interleave([pipelined(diag + first), gla_steps(), gate_steps()])
    ga = jnp.concatenate(gates[:GATE_SLABS], axis=-1)
    gb = jnp.concatenate(gates[GATE_SLABS:], axis=-1)
    oa = gla_out[0] * _silu(ga)

    @pl.when(t == nt - 1)
    def _emit_state():
        for hh in range(GLA_HEADS):
            so_ref[0, hh] = sst[hh * GLA_DK:(hh + 1) * GLA_DK, :]

    def still_live():
        lowest = car[0]
        for hh in range(1, SB_HEADS):
            lowest = jnp.minimum(lowest, car[hh])
        return (jnp.min(lowest) < SB_UNDERFLOW_LOG2).astype(jnp.int32)

    def more(c):
        return jnp.logical_and(c[0] < nblk, c[1] > 0)

    def below(c):
        i = c[0]
        interleave([pipelined([below_block(nblk - 1 - i - u) for u in range(step)])])
        return i + step, still_live()

    lax.while_loop(more, below, (jnp.int32(step), still_live()))

    ob = jnp.concatenate([acc[p] for p in range(n_pairs)], axis=-1)
    ob = ob * _silu(gb)

    mix = jnp.concatenate([oa, ob], axis=-1).astype(BF16)
    x = x + _dot(mix, wout_ref[...])
    xo_ref[0] = _cross_attend(x, gcr_ref[...], wq_ref, wo_ref, [(mkb, mvb)])


def _const_spec(stacked, index):
    tail = stacked.shape[1:]
    return pl.BlockSpec((1,) + tail, lambda *_: (index,) + (0,) * len(tail), pipeline_mode=pl.Buffered(1))


def _even_layer(x, past_k, past_v, s0, prev_kv, mem_k, mem_v, layer, wts, tq):
    bsz, t, d = x.shape
    past = 0 if past_k is None else past_k.shape[-1]
    has_state = s0 is not None
    nmem = mem_k.shape[2]
    li = layer // 2
    kv_major = tq % LANES == 0
    n_prev = 0 if prev_kv is None else prev_kv[0].shape[0]
    assert kv_major or prev_kv is None
    assert t % tq == 0 and tq % GLA_CHUNK == 0 and past % SB_KEYS == 0
    assert (tq % SB_KEYS == 0 or t == tq) and SB_KEYS == LANES

    tile = lambda b, i: (b, i, 0)
    args, specs = [x], [pl.BlockSpec((1, tq, d), tile)]
    if past:
        args += [past_k, past_v]
        specs += [pl.BlockSpec((1, 1, SB_W, past), lambda b, i: (li, b, 0, 0))] * 2
    if has_state:
        args.append(s0)
        specs.append(pl.BlockSpec((1, 1, GLA_HEADS, GLA_DK, GLA_DV), lambda b, i: (li, b, 0, 0, 0)))
    if n_prev:
        args += list(prev_kv)
        specs += [pl.BlockSpec((n_prev, 1, SB_W, tq), lambda b, i: (0, b, 0, i))] * 2
    args += [mem_k, mem_v]
    specs += [pl.BlockSpec((1, 1, nmem, d), lambda b, i: (layer, b, 0, 0))] * 2
    for name in ("g_mix", "w_in_a", "w_in_b", "w_in_r", "w_alpha", "b_alpha", "g_gla", "w_out", "g_cross", "w_cq", "w_co"):
        stacked, index = wts[name]
        args.append(stacked)
        specs.append(_const_spec(stacked, index))

    if kv_major:
        kv_shape = jax.ShapeDtypeStruct((n_prev + 1, bsz, SB_W, t), F32)
        kv_spec = pl.BlockSpec((n_prev + 1, 1, SB_W, tq), lambda b, i: (0, b, 0, i))
    else:
        kv_shape = jax.ShapeDtypeStruct((bsz, t, SB_W), F32)
        kv_spec = pl.BlockSpec((1, tq, SB_W), tile)
    out_shape = [jax.ShapeDtypeStruct((bsz, t, d), F32), kv_shape, kv_shape,
                 jax.ShapeDtypeStruct((bsz, GLA_HEADS, GLA_DK, GLA_DV), F32)]
    out_specs = [pl.BlockSpec((1, tq, d), tile), kv_spec, kv_spec,
                 pl.BlockSpec((1, GLA_HEADS, GLA_DK, GLA_DV), lambda b, i: (b, 0, 0, 0))]
    hist = past + pl.cdiv(t, SB_KEYS) * SB_KEYS
    scratch = [pltpu.VMEM((2, hist, SB_W), BF16), pltpu.VMEM((2, hist, SB_W), BF16),
               pltpu.VMEM((nmem, d), BF16), pltpu.VMEM((nmem, d), BF16),
               pltpu.VMEM((GLA_KW, GLA_DV), F32),
               pltpu.VMEM((tq, SB_W), BF16),
               pltpu.VMEM((SB_HEADS // 2, tq, LANES), F32),
               pltpu.VMEM((SB_HEADS, tq, LANES), F32)]
    return pl.pallas_call(
        functools.partial(_even_kernel, tq=tq, nt_static=t // tq, past=past, has_state=has_state,
                          n_prev=n_prev, kv_major=kv_major),
        grid=(bsz, t // tq),
        in_specs=specs, out_specs=out_specs, out_shape=out_shape, scratch_shapes=scratch,
        compiler_params=pltpu.CompilerParams(
            dimension_semantics=("arbitrary", "arbitrary"), vmem_limit_bytes=VMEM_LIMIT),
        name="even_layer",
    )(*args)


def _odd_kernel(*refs, tq, n_rows, cl, emit_vn, final):
    it = iter(refs)
    x_ref, mk_ref, mv_ref = next(it), next(it), next(it)
    gmix_ref, w_ref, gv_ref, bv_ref, wsp_ref, bsp_ref, wout_ref, gcr_ref, wq_ref, wo_ref = (
        next(it).at[0] for _ in range(10))
    gfin_ref = next(it).at[0] if final else None
    xo_ref = next(it)
    vn_ref = next(it) if emit_vn else None
    mkb, mvb = next(it), next(it)

    @pl.when(pl.program_id(1) == 0)
    def _init():
        mkb[...] = mk_ref[0].astype(BF16)
        mvb[...] = mv_ref[0].astype(BF16)

    d = x_ref.shape[-1]
    x = x_ref[...].reshape(n_rows * tq, d)
    h = _rms(x, gmix_ref[...]).astype(BF16)
    u = _dot(h, w_ref[:, 0:d])
    v = _dot(h, w_ref[:, d:2 * d])
    g = _dot(h, w_ref[:, 2 * d:3 * d])

    gv = _gelu(v)
    xc = gv - jnp.mean(gv, axis=-1, keepdims=True)
    v_n = xc * lax.rsqrt(jnp.mean(xc * xc, axis=-1, keepdims=True) + EPS) * gv_ref[...] + bv_ref[...]
    if emit_vn:
        vn_ref[...] = v_n.reshape(n_rows, tq, d)

    gw = d // SGU_GROUPS
    keep = _iota((cl, cl), 1) <= _iota((cl, cl), 0)
    vnb = v_n.astype(BF16)
    bsp = bsp_ref[0:cl, :]
    wgs = [jnp.where(keep, wsp_ref[gi, 0:cl, 0:cl], 0.0).astype(BF16) for gi in range(SGU_GROUPS)]
    rows = []
    for c in range(n_rows * tq // cl):
        cols = []
        for gi in range(SGU_GROUPS):
            cols.append(_dot(wgs[gi], vnb[c * cl:(c + 1) * cl, gi * gw:(gi + 1) * gw]) + bsp[:, gi:gi + 1])
        rows.append(jnp.concatenate(cols, axis=-1))
    s = jnp.concatenate(rows, axis=0) if len(rows) > 1 else rows[0]

    y = (_gelu(u) * s * _silu(g)).astype(BF16)
    x = x + _dot(y, wout_ref[...])
    x = _cross_attend(x, gcr_ref[...], wq_ref, wo_ref, [(mkb.at[r], mvb.at[r]) for r in range(n_rows)])
    if final:
        x = _rms(x, gfin_ref[...])
    xo_ref[...] = x.reshape(n_rows, tq, d)


def _odd_layer(x, mem_k, mem_v, layer, wts, tq, emit_vn, g_final):
    bsz, t, d = x.shape
    nmem = mem_k.shape[2]
    cl = min(SGU_CHUNK, t)
    final = g_final is not None
    assert t % tq == 0 and tq % cl == 0
    rows = ODD_ROWS if (t == tq and bsz % ODD_ROWS == 0 and ODD_ROWS * tq <= ODD_TILE) else 1

    tile = lambda b, i: (b, i, 0)
    args = [x, mem_k, mem_v]
    specs = [pl.BlockSpec((rows, tq, d), tile)]
    specs += [pl.BlockSpec((1, rows, nmem, d), lambda b, i: (layer, b, 0, 0))] * 2
    consts = [wts[name] for name in
              ("g_mix", "w_in", "g_sgu", "b_sgu", "w_sp", "b_sp_t", "w_out", "g_cross", "w_cq", "w_co")]
    if final:
        consts.append((g_final, 0))
    for stacked, index in consts:
        args.append(stacked)
        specs.append(_const_spec(stacked, index))

    out_shape = [jax.ShapeDtypeStruct((bsz, t, d), F32)]
    out_specs = [pl.BlockSpec((rows, tq, d), tile)]
    if emit_vn:
        out_shape.append(jax.ShapeDtypeStruct((bsz, t, d), F32))
        out_specs.append(pl.BlockSpec((rows, tq, d), tile))
    return pl.pallas_call(
        functools.partial(_odd_kernel, tq=tq, n_rows=rows, cl=cl, emit_vn=emit_vn, final=final),
        grid=(bsz // rows, t // tq),
        in_specs=specs, out_specs=out_specs, out_shape=out_shape,
        scratch_shapes=[pltpu.VMEM((rows, nmem, d), BF16), pltpu.VMEM((rows, nmem, d), BF16)],
        compiler_params=pltpu.CompilerParams(
            dimension_semantics=("arbitrary", "arbitrary"), vmem_limit_bytes=VMEM_LIMIT),
        name="odd_layer",
    )(*args)


EVEN_TILE = 256
MEM_ROWS = 4
ODD_ROWS = 4
ODD_TILE = 1024


def _tile_rows(t, layer):
    return min(EVEN_TILE if layer % 2 == 0 else ODD_TILE, t)


def _run_group(x, past_k, past_v, s0, mem_k, mem_v, even_w, odd_w, g_final, emit_vn):
    depth = mem_k.shape[0]
    bsz, t, d = x.shape
    tq = _tile_rows(t, 0)
    mem_k = mem_k.reshape(depth, bsz, -1, d)
    mem_v = mem_v.reshape(depth, bsz, -1, d)
    if past_k is not None:
        n_even, _, past = past_k.shape[:3]
        past_k = past_k.transpose(0, 1, 3, 4, 2).reshape(n_even, bsz, SB_W, past)
        past_v = past_v.transpose(0, 1, 3, 4, 2).reshape(n_even, bsz, SB_W, past)
    kv_major = tq % LANES == 0
    sb_k, sb_v, gla_s, sgu_v = [], [], [], []
    prev_kv = None
    for l in range(depth):
        if l % 2 == 0:
            x, kb, vb, s_new = _even_layer(x, past_k, past_v, s0, prev_kv, mem_k, mem_v, l, even_w[l], tq)
            if kv_major:
                prev_kv = (kb, vb)
            else:
                sb_k.append(kb.reshape(bsz, t, SB_HEADS, SB_DIM))
                sb_v.append(vb.reshape(bsz, t, SB_HEADS, SB_DIM))
            gla_s.append(s_new)
        else:
            outs = _odd_layer(x, mem_k, mem_v, l, odd_w[l], _tile_rows(t, l), emit_vn,
                              g_final if l == depth - 1 else None)
            x = outs[0]
            if emit_vn:
                sgu_v.append(outs[1])
    if kv_major:
        sb_k, sb_v = (a.reshape(a.shape[0], bsz, SB_HEADS, SB_DIM, t).transpose(0, 1, 4, 2, 3) for a in prev_kv)
    else:
        sb_k, sb_v = jnp.stack(sb_k), jnp.stack(sb_v)
    return x, sb_k, sb_v, jnp.stack(gla_s), (jnp.stack(sgu_v) if emit_vn else None)


def kernel(x_prompt, x_sample, cache_sb_k, cache_sb_v, state_gla, cache_mem_k, cache_mem_v, mem_prompt, g_mix, w_in_even, w_alpha, b_alpha, g_gla_out, w_out_even, w_in_odd, g_sgu_v, b_sgu_v, w_sp, b_sp, w_out_odd, g_cross, g_mem, w_cq, w_ck, w_cv, w_co, g_final):
    depth, d = g_mix.shape
    vec = lambda a: a[:, None, :]

    o_ra = C_QB
    w_in_r = jnp.concatenate(
        [w_in_even[:, :, o_ra:o_ra + GLA_RANK],
         jnp.zeros(w_in_even.shape[:2] + (RA_PAD - GLA_RANK,), w_in_even.dtype)], axis=2).astype(BF16)
    w_al = jnp.concatenate(
        [w_alpha, jnp.zeros((w_alpha.shape[0], RA_PAD - GLA_RANK, GLA_KW), w_alpha.dtype)], axis=1).astype(BF16)
    stacks = dict(
        g_mix=vec(g_mix), g_cross=vec(g_cross), w_cq=w_cq.astype(BF16), w_co=w_co.astype(BF16),
        w_in_a=w_in_even[:, :, :o_ra].astype(BF16), w_in_b=w_in_even[:, :, o_ra + GLA_RANK:].astype(BF16),
        w_in_r=w_in_r, w_alpha=w_al, b_alpha=vec(b_alpha), g_gla=vec(g_gla_out), w_out_e=w_out_even.astype(BF16),
        w_in_o=w_in_odd.astype(BF16), g_sgu=vec(g_sgu_v), b_sgu=vec(b_sgu_v), w_sp=w_sp,
        b_sp_t=b_sp.transpose(0, 2, 1), w_out_o=w_out_odd.astype(BF16))
    even_w, odd_w = {}, {}
    for l in range(depth):
        i = l // 2
        shared = {k: (stacks[k], l) for k in ("g_mix", "g_cross", "w_cq", "w_co")}
        if l % 2 == 0:
            even_w[l] = dict(shared, w_in_a=(stacks["w_in_a"], i), w_in_b=(stacks["w_in_b"], i),
                             w_in_r=(stacks["w_in_r"], i), w_alpha=(stacks["w_alpha"], i),
                             b_alpha=(stacks["b_alpha"], i), g_gla=(stacks["g_gla"], i),
                             w_out=(stacks["w_out_e"], i))
        else:
            odd_w[l] = dict(shared, w_in=(stacks["w_in_o"], i), g_sgu=(stacks["g_sgu"], i),
                            b_sgu=(stacks["b_sgu"], i), w_sp=(stacks["w_sp"], i),
                            b_sp_t=(stacks["b_sp_t"], i), w_out=(stacks["w_out_o"], i))

    mem_k_p, mem_v_p, mem_k_out, mem_v_out = _memkv(mem_prompt, g_mem, w_ck.astype(BF16), w_cv.astype(BF16))
    gfin = g_final.reshape(1, 1, -1)

    y_p, sbk_p, sbv_p, gla_p, _ = _run_group(
        x_prompt, None, None, None, mem_k_p, mem_v_p, even_w, odd_w, gfin, False)
    y_s, sbk_s, sbv_s, gla_s, sgu_s = _run_group(
        x_sample, cache_sb_k, cache_sb_v, state_gla, cache_mem_k, cache_mem_v, even_w, odd_w, gfin, True)

    return (y_p, y_s, sbk_p, sbv_p, gla_p, mem_k_out, mem_v_out, sbk_s, sbv_s, gla_s, sgu_s)
```
